```python
import math
import jax, jax.numpy as jnp
from jax import lax
import numpy as np

D_MODEL = 1024
BATCH = 8
SEQ = 2048
DEPTH = 2
DEC_BATCH = 128
DEC_SEQ = 1
PAST_LEN = 2048
PAGE_SIZE = 128

HEAD_DIM = 64
N_HEADS = D_MODEL // HEAD_DIM
H_SB = N_HEADS // 2
H_MOBA = N_HEADS - H_SB
H_FOX = N_HEADS
D_MIX = N_HEADS * HEAD_DIM
MOBA_BLOCK = 256
MOBA_TOPK = 3
MOBA_QBLK = 32
Q_BLOCK = 128
REL_BUCKETS = 32
REL_MAX_DIST = 128
D_FF = ((8 * D_MODEL + 3 * 256 - 1) // (3 * 256)) * 256
N_EVEN = (DEPTH + 1) // 2
N_ODD = DEPTH // 2
EPS = 1e-6
SCALE = HEAD_DIM ** -0.5

kernel_name = "hybrid_stickbreak_moba_fox_step"


def rms_norm(x, g):
    xf = x.astype(jnp.float32)
    y = xf * lax.rsqrt(jnp.mean(xf * xf, axis=-1, keepdims=True) + EPS)
    return (y * g.astype(jnp.float32)).astype(x.dtype)


def swiglu(h, w_gate, w_up, w_down):
    return (jax.nn.silu(h @ w_gate) * (h @ w_up)) @ w_down


def split_heads(u, n_heads):
    return u.reshape(*u.shape[:-1], n_heads, HEAD_DIM)


def gather_pages(pool, page_table):
    g = pool[page_table]
    return g.reshape(g.shape[0], g.shape[1] * g.shape[2], *g.shape[3:])


def rel_bucket(dist):
    max_exact = REL_BUCKETS // 2
    d = jnp.maximum(dist, 0)
    df = jnp.maximum(d, 1).astype(jnp.float32)
    large = max_exact + (jnp.log(df / max_exact) / math.log(REL_MAX_DIST / max_exact)
                         * (REL_BUCKETS - max_exact)).astype(jnp.int32)
    large = jnp.minimum(large, REL_BUCKETS - 1)
    return jnp.where(d < max_exact, d, large)


def query_sweep(fn, qs, q_pos, blk):
    B, T = qs[0].shape[:2]
    n = T // blk
    to_blocks = lambda a: jnp.moveaxis(a.reshape(B, n, blk, *a.shape[2:]), 1, 0)
    xs = tuple(to_blocks(a) for a in qs) + (q_pos.reshape(n, blk),)
    ob = lax.map(lambda args: fn(*args), xs)
    return jnp.moveaxis(ob, 0, 1).reshape(B, T, *ob.shape[3:])


def stick_breaking(q, q_pos, k, v, k_pos):
    z = jnp.einsum('bqhd,bkhd->bhqk', q, k).astype(jnp.float32) * SCALE
    mask = k_pos[None, :] < q_pos[:, None]
    log_beta = jax.nn.log_sigmoid(z)
    log_1m = jnp.where(mask, jax.nn.log_sigmoid(-z), 0.0)
    tail = lax.cumsum(log_1m, axis=3, reverse=True) - log_1m
    w = jnp.where(mask, jnp.exp(log_beta + tail), 0.0)
    return jnp.einsum('bhqk,bkhd->bqhd', w.astype(v.dtype), v)


def moba_blocks(k, v):
    B, T, H, d = k.shape
    nb = -(-T // MOBA_BLOCK)
    pad = ((0, 0), (0, nb * MOBA_BLOCK - T), (0, 0), (0, 0))
    kbt = jnp.pad(k, pad).reshape(B, nb, MOBA_BLOCK, H, d).transpose(0, 3, 1, 2, 4)
    vbt = jnp.pad(v, pad).reshape(B, nb, MOBA_BLOCK, H, d).transpose(0, 3, 1, 2, 4)
    kmean = jnp.mean(kbt.astype(jnp.float32), axis=3)
    return kbt, vbt, kmean


def moba_attend(q, q_pos, kbt, vbt, kmean, rel_bias):
    B, H, nb = kmean.shape[:3]
    Tq = q.shape[1]
    own = q_pos // MOBA_BLOCK
    qh = q.transpose(0, 2, 1, 3)
    score = jnp.einsum('bhqd,bhnd->bhqn', qh.astype(jnp.float32), kmean)
    fully_past = jnp.arange(nb)[None, :] < own[:, None]
    score = jnp.where(fully_past, score, -jnp.inf)
    if nb < MOBA_TOPK:
        score = jnp.pad(score, ((0, 0), (0, 0), (0, 0), (0, MOBA_TOPK - nb)), constant_values=-jnp.inf)
    _, top_idx = lax.top_k(score, MOBA_TOPK)
    slot_ok = jnp.arange(MOBA_TOPK)[None, :] < own[:, None]
    top_idx = jnp.where(slot_ok, top_idx, 0)
    own_idx = jnp.broadcast_to(own[None, None, :, None], (B, H, Tq, 1)).astype(top_idx.dtype)
    idx = jnp.concatenate([top_idx, own_idx], axis=-1)
    slot_ok = jnp.concatenate([slot_ok, jnp.ones((Tq, 1), dtype=bool)], axis=-1)
    bi = jnp.arange(B)[:, None, None, None]
    hi = jnp.arange(H)[None, :, None, None]
    kg = kbt[bi, hi, idx]
    vg = vbt[bi, hi, idx]
    key_pos = idx[..., None] * MOBA_BLOCK + jnp.arange(MOBA_BLOCK)
    qp = q_pos[None, None, :, None, None]
    allowed = slot_ok[None, None, :, :, None] & (key_pos <= qp)
    bias = rel_bias.T[hi[..., None], rel_bucket(qp - key_pos)]
    logits = jnp.einsum('bhqd,bhqjkd->bhqjk', qh, kg).astype(jnp.float32) * SCALE + bias.astype(jnp.float32)
    logits = jnp.where(allowed, logits, -jnp.inf)
    p = jax.nn.softmax(logits.reshape(B, H, Tq, -1), axis=-1).reshape(logits.shape)
    return jnp.einsum('bhqjk,bhqjkd->bqhd', p.astype(vg.dtype), vg)


def forgetting_attn(q, c_q, q_pos, k, v, c_k, k_pos):
    logits = jnp.einsum('bqhd,bkhd->bhqk', q, k).astype(jnp.float32) * SCALE
    logits = logits + (jnp.swapaxes(c_q, 1, 2)[..., :, None] - jnp.swapaxes(c_k, 1, 2)[..., None, :])
    logits = jnp.where(k_pos[None, :] <= q_pos[:, None], logits, -jnp.inf)
    p = jax.nn.softmax(logits, axis=-1)
    return jnp.einsum('bhqk,bkhd->bqhd', p.astype(v.dtype), v)


def even_projection(h, w_in, g_q, g_k):
    sa, sb = H_SB * HEAD_DIM, H_MOBA * HEAD_DIM
    u = h @ w_in
    q_sb, k_sb, v_sb, q_mb, k_mb, v_mb = jnp.split(
        u, [sa, 2 * sa, 3 * sa, 3 * sa + sb, 3 * sa + 2 * sb], axis=-1)
    q_sb, k_sb, v_sb = split_heads(q_sb, H_SB), split_heads(k_sb, H_SB), split_heads(v_sb, H_SB)
    q_mb = rms_norm(split_heads(q_mb, H_MOBA), g_q)
    k_mb = rms_norm(split_heads(k_mb, H_MOBA), g_k)
    v_mb = split_heads(v_mb, H_MOBA)
    k = jnp.concatenate([k_sb, k_mb], axis=2)
    v = jnp.concatenate([v_sb, v_mb], axis=2)
    return q_sb, q_mb, k, v


def even_attention(q_sb, q_mb, k, v, q_pos, rel_bias, sweep):
    k_pos = jnp.arange(k.shape[1], dtype=jnp.int32)
    k_sb, k_mb = k[:, :, :H_SB], k[:, :, H_SB:]
    v_sb, v_mb = v[:, :, :H_SB], v[:, :, H_SB:]
    kbt, vbt, kmean = moba_blocks(k_mb, v_mb)
    sb_fn = lambda qq, pp: stick_breaking(qq, pp, k_sb, v_sb, k_pos)
    mb_fn = lambda qq, pp: moba_attend(qq, pp, kbt, vbt, kmean, rel_bias)
    if sweep:
        o_sb = query_sweep(sb_fn, (q_sb,), q_pos, Q_BLOCK)
        o_mb = query_sweep(mb_fn, (q_mb,), q_pos, MOBA_QBLK)
    else:
        o_sb = sb_fn(q_sb, q_pos)
        o_mb = mb_fn(q_mb, q_pos)
    o = jnp.concatenate([o_sb, o_mb], axis=2)
    return o.reshape(*o.shape[:2], D_MIX)


def odd_projection(h, w_in, b_f, g_q, g_k):
    s = H_FOX * HEAD_DIM
    u = h @ w_in
    q, k, v, fg = jnp.split(u, [s, 2 * s, 3 * s], axis=-1)
    q = rms_norm(split_heads(q, H_FOX), g_q)
    k = rms_norm(split_heads(k, H_FOX), g_k)
    v = split_heads(v, H_FOX)
    log_f = jax.nn.log_sigmoid((fg + b_f).astype(jnp.float32))
    return q, k, v, log_f


def odd_attention(q, c_q, k, v, c_k, q_pos, sweep):
    k_pos = jnp.arange(k.shape[1], dtype=jnp.int32)
    fn = lambda qq, cc, pp: forgetting_attn(qq, cc, pp, k, v, c_k, k_pos)
    o = query_sweep(fn, (q, c_q), q_pos, Q_BLOCK) if sweep else fn(q, c_q, q_pos)
    return o.reshape(*o.shape[:2], D_MIX)


def setup_inputs(seed: int = 0) -> dict:
    key = jax.random.key(seed)
    ks = jax.random.split(key, 24)
    n_pages = PAST_LEN // PAGE_SIZE
    n_used = DEC_BATCH * n_pages
    n_pool = n_used + -(-n_used // 4)
    nrm = lambda k, shape, scale=1.0: jax.random.normal(k, shape, jnp.float32) * scale
    d_in_even = 3 * (H_SB + H_MOBA) * HEAD_DIM
    d_in_odd = 3 * H_FOX * HEAD_DIM + H_FOX
    page_table = jax.random.permutation(ks[7], n_pool)[:n_used].reshape(DEC_BATCH, n_pages).astype(jnp.int32)
    return {
        "x_prompt": nrm(ks[0], (BATCH, SEQ, D_MODEL)),
        "x_sample": nrm(ks[1], (DEC_BATCH, DEC_SEQ, D_MODEL)),
        "cache_k_even": nrm(ks[2], (N_EVEN, n_pool, PAGE_SIZE, N_HEADS, HEAD_DIM)),
        "cache_v_even": nrm(ks[3], (N_EVEN, n_pool, PAGE_SIZE, N_HEADS, HEAD_DIM)),
        "cache_k_odd": nrm(ks[4], (N_ODD, n_pool, PAGE_SIZE, H_FOX, HEAD_DIM)),
        "cache_v_odd": nrm(ks[5], (N_ODD, n_pool, PAGE_SIZE, H_FOX, HEAD_DIM)),
        "cache_logf_odd": jax.nn.log_sigmoid(nrm(ks[6], (N_ODD, n_pool, PAGE_SIZE, H_FOX)) + 3.0),
        "page_table": page_table,
        "attn_norm": 1.0 + nrm(ks[8], (DEPTH, D_MODEL), 0.02),
        "ffn_norm": 1.0 + nrm(ks[9], (DEPTH, D_MODEL), 0.02),
        "w_in_even": nrm(ks[10], (N_EVEN, D_MODEL, d_in_even), D_MODEL ** -0.5),
        "w_out_even": nrm(ks[11], (N_EVEN, D_MIX, D_MODEL), D_MIX ** -0.5),
        "q_norm_even": 1.0 + nrm(ks[12], (N_EVEN, HEAD_DIM), 0.02),
        "k_norm_even": 1.0 + nrm(ks[13], (N_EVEN, HEAD_DIM), 0.02),
        "rel_bias": nrm(ks[14], (REL_BUCKETS, H_MOBA), 0.5),
        "w_in_odd": nrm(ks[15], (N_ODD, D_MODEL, d_in_odd), D_MODEL ** -0.5),
        "b_f_odd": jax.random.uniform(ks[16], (N_ODD, H_FOX), jnp.float32, 1.0, 5.0),
        "w_out_odd": nrm(ks[17], (N_ODD, D_MIX, D_MODEL), D_MIX ** -0.5),
        "q_norm_odd": 1.0 + nrm(ks[18], (N_ODD, HEAD_DIM), 0.02),
        "k_norm_odd": 1.0 + nrm(ks[19], (N_ODD, HEAD_DIM), 0.02),
        "w_gate": nrm(ks[20], (DEPTH, D_MODEL, D_FF), D_MODEL ** -0.5),
        "w_up": nrm(ks[21], (DEPTH, D_MODEL, D_FF), D_MODEL ** -0.5),
        "w_down": nrm(ks[22], (DEPTH, D_FF, D_MODEL), D_FF ** -0.5),
    }


def reference(x_prompt, x_sample, cache_k_even, cache_v_even, cache_k_odd, cache_v_odd, cache_logf_odd,
              page_table, attn_norm, ffn_norm, w_in_even, w_out_even, q_norm_even, k_norm_even, rel_bias,
              w_in_odd, b_f_odd, w_out_odd, q_norm_odd, k_norm_odd, w_gate, w_up, w_down):
    seq = x_prompt.shape[1]
    dec_seq = x_sample.shape[1]
    past_len = page_table.shape[1] * PAGE_SIZE
    pos_p = jnp.arange(seq, dtype=jnp.int32)
    pos_s = past_len + jnp.arange(dec_seq, dtype=jnp.int32)
    xp, xs = x_prompt, x_sample
    ek_p, ev_p, ek_s, ev_s = [], [], [], []
    ok_p, ov_p, of_p, ok_s, ov_s, of_s = [], [], [], [], [], []
    for l in range(DEPTH):
        i = l // 2
        hp = rms_norm(xp, attn_norm[l])
        hs = rms_norm(xs, attn_norm[l])
        if l % 2 == 0:
            q_sb, q_mb, k, v = even_projection(hp, w_in_even[i], q_norm_even[i], k_norm_even[i])
            op = even_attention(q_sb, q_mb, k, v, pos_p, rel_bias, True)
            ek_p.append(k)
            ev_p.append(v)
            q_sb, q_mb, k, v = even_projection(hs, w_in_even[i], q_norm_even[i], k_norm_even[i])
            k_all = jnp.concatenate([gather_pages(cache_k_even[i], page_table), k], axis=1)
            v_all = jnp.concatenate([gather_pages(cache_v_even[i], page_table), v], axis=1)
            os_ = even_attention(q_sb, q_mb, k_all, v_all, pos_s, rel_bias, False)
            ek_s.append(k)
            ev_s.append(v)
            xp = xp + op @ w_out_even[i]
            xs = xs + os_ @ w_out_even[i]
        else:
            q, k, v, lf = odd_projection(hp, w_in_odd[i], b_f_odd[i], q_norm_odd[i], k_norm_odd[i])
            c = jnp.cumsum(lf, axis=1)
            op = odd_attention(q, c, k, v, c, pos_p, True)
            ok_p.append(k)
            ov_p.append(v)
            of_p.append(lf)
            q, k, v, lf = odd_projection(hs, w_in_odd[i], b_f_odd[i], q_norm_odd[i], k_norm_odd[i])
            k_all = jnp.concatenate([gather_pages(cache_k_odd[i], page_table), k], axis=1)
            v_all = jnp.concatenate([gather_pages(cache_v_odd[i], page_table), v], axis=1)
            lf_all = jnp.concatenate([gather_pages(cache_logf_odd[i], page_table).astype(jnp.float32), lf], axis=1)
            c = jnp.cumsum(lf_all, axis=1)
            os_ = odd_attention(q, c[:, past_len:], k_all, v_all, c, pos_s, False)
            ok_s.append(k)
            ov_s.append(v)
            of_s.append(lf)
            xp = xp + op @ w_out_odd[i]
            xs = xs + os_ @ w_out_odd[i]
        xp = xp + swiglu(rms_norm(xp, ffn_norm[l]), w_gate[l], w_up[l], w_down[l])
        xs = xs + swiglu(rms_norm(xs, ffn_norm[l]), w_gate[l], w_up[l], w_down[l])
    return (xp, xs,
            jnp.stack(ek_p), jnp.stack(ev_p), jnp.stack(ek_s), jnp.stack(ev_s),
            jnp.stack(ok_p), jnp.stack(ov_p), jnp.stack(of_p),
            jnp.stack(ok_s), jnp.stack(ov_s), jnp.stack(of_s))
```

```python
import functools
import math

import jax
import jax.numpy as jnp
from jax import lax
from jax.experimental import pallas as pl
from jax.experimental.pallas import tpu as pltpu

F32 = jnp.float32
BF16 = jnp.bfloat16

HEAD_DIM = 64
HEAD_SHIFT = HEAD_DIM.bit_length() - 1
EPS = 1e-6
SCALE = HEAD_DIM ** -0.5
MOBA_BLOCK = 256
MOBA_TOPK = 3
PAGE_SIZE = 128
REL_BUCKETS = 32
REL_MAX_DIST = 128
NEG = -1e30
LANES = 128
MXU_DIM = 256
VMEM_LIMIT = 56 * 1024 * 1024

NT = (((1,), (1,)), ((), ()))


def _bucket_thresholds():
    max_exact = REL_BUCKETS // 2
    def bucket(d):
        return min(max_exact + int(math.log(d / max_exact) / math.log(REL_MAX_DIST / max_exact)
                                   * (REL_BUCKETS - max_exact)), REL_BUCKETS - 1)
    return [min(d for d in range(max_exact, REL_MAX_DIST + 1) if bucket(d) >= b)
            for b in range(max_exact + 1, REL_BUCKETS)]


_BUCKET_THRESHOLDS = _bucket_thresholds()


def _params(*sem):
    return pltpu.CompilerParams(dimension_semantics=sem, vmem_limit_bytes=VMEM_LIMIT)


def _const_spec(shape):
    return pl.BlockSpec(shape, lambda *_: (0,) * len(shape), pipeline_mode=pl.Buffered(1))


def _log_sigmoid_parts(z):
    sp = jnp.log1p(jnp.exp(-jnp.abs(z)))
    return jnp.minimum(z, 0.0) - sp, jnp.minimum(-z, 0.0) - sp


def _split2(x):
    hi = x.astype(BF16)
    lo = (x - hi.astype(F32)).astype(BF16)
    return hi, lo


def _split3(x):
    a = x.astype(BF16)
    r = x - a.astype(F32)
    b = r.astype(BF16)
    c = (r - b.astype(F32)).astype(BF16)
    return a, b, c


def _dot(a, b):
    return jnp.dot(a, b, preferred_element_type=F32)


def _dot_nt(a, b):
    return lax.dot_general(a, b, NT, preferred_element_type=F32)


def _rms(x, g):
    ms = jnp.mean(x * x, axis=-1, keepdims=True)
    return x * lax.rsqrt(ms + EPS) * g


def _head_rms(t, gain, seg):
    outs = []
    for c in range(t.shape[1] // MXU_DIM):
        tc = t[:, c * MXU_DIM:(c + 1) * MXU_DIM]
        hi, lo = _split2(tc * tc)
        ms = _dot(hi, seg) + _dot(lo, seg)
        outs.append(tc * lax.rsqrt(ms + EPS) * gain[:, c * MXU_DIM:(c + 1) * MXU_DIM])
    return jnp.concatenate(outs, axis=1)


def _proj_even_kernel(x_ref, g_ref, w_ref, gq_ref, gk_ref, seg_ref,
                      k_ref, v_ref, qb_ref, kb_ref, vb_ref, qf_ref, *, half):
    h = _rms(x_ref[...], g_ref[...]).astype(BF16)
    u = _dot(h, w_ref[...])
    seg = seg_ref[...]
    q_sb, k_sb, v_sb = u[:, 0:half], u[:, half:2 * half], u[:, 2 * half:3 * half]
    q_mb = _head_rms(u[:, 3 * half:4 * half], gq_ref[...], seg)
    k_mb = _head_rms(u[:, 4 * half:5 * half], gk_ref[...], seg)
    v_mb = u[:, 5 * half:6 * half]
    k_ref[:, 0:half] = k_sb
    k_ref[:, half:] = k_mb
    v_ref[:, 0:half] = v_sb
    v_ref[:, half:] = v_mb
    qb_ref[:, 0:half] = (q_sb * SCALE).astype(BF16)
    qb_ref[:, half:] = (q_mb * SCALE).astype(BF16)
    kb_ref[:, 0:half] = k_sb.astype(BF16)
    kb_ref[:, half:] = k_mb.astype(BF16)
    vb_ref[:, 0:half] = v_sb.astype(BF16)
    vb_ref[:, half:] = v_mb.astype(BF16)
    qf_ref[...] = q_mb


def _proj_even(x, g, w_bf, gq, gk, seg):
    rows, d = x.shape
    dm = w_bf.shape[1] // 3
    half = dm // 2
    tm = min(256, rows)
    row = lambda w: pl.BlockSpec((tm, w), lambda i: (i, 0))
    return pl.pallas_call(
        functools.partial(_proj_even_kernel, half=half),
        grid=(rows // tm,),
        in_specs=[row(d), _const_spec((1, d)), _const_spec(w_bf.shape), _const_spec((1, half)),
                  _const_spec((1, half)), _const_spec(seg.shape)],
        out_specs=[row(dm), row(dm), row(dm), row(dm), row(dm), row(half)],
        out_shape=[jax.ShapeDtypeStruct((rows, dm), F32), jax.ShapeDtypeStruct((rows, dm), F32),
                   jax.ShapeDtypeStruct((rows, dm), BF16), jax.ShapeDtypeStruct((rows, dm), BF16),
                   jax.ShapeDtypeStruct((rows, dm), BF16), jax.ShapeDtypeStruct((rows, half), F32)],
        compiler_params=_params("parallel"),
        name="proj_even",
    )(x, g, w_bf, gq, gk, seg)


def _proj_odd_kernel(x_ref, g_ref, w_ref, gq_ref, gk_ref, bf_ref, seg_ref,
                     k_ref, v_ref, qb_ref, kb_ref, vb_ref, lf_ref, *, dm, nh):
    h = _rms(x_ref[...], g_ref[...]).astype(BF16)
    u = _dot(h, w_ref[...])
    seg = seg_ref[...]
    q = _head_rms(u[:, 0:dm], gq_ref[...], seg)
    k = _head_rms(u[:, dm:2 * dm], gk_ref[...], seg)
    v = u[:, 2 * dm:3 * dm]
    k_ref[...] = k
    v_ref[...] = v
    qb_ref[...] = (q * SCALE).astype(BF16)
    kb_ref[...] = k.astype(BF16)
    vb_ref[...] = v.astype(BF16)
    lf, _ = _log_sigmoid_parts(u[:, 3 * dm:] + bf_ref[...])
    lf_ref[...] = lf[:, 0:nh]


def _proj_odd(x, g, w_bf, gq, gk, b_f, seg, nh):
    rows, d = x.shape
    dm = (w_bf.shape[1] - LANES) // 3
    tm = min(256, rows)
    row = lambda w: pl.BlockSpec((tm, w), lambda i: (i, 0))
    return pl.pallas_call(
        functools.partial(_proj_odd_kernel, dm=dm, nh=nh),
        grid=(rows // tm,),
        in_specs=[row(d), _const_spec((1, d)), _const_spec(w_bf.shape), _const_spec((1, dm)),
                  _const_spec((1, dm)), _const_spec((1, LANES)), _const_spec(seg.shape)],
        out_specs=[row(dm), row(dm), row(dm), row(dm), row(dm), row(nh)],
        out_shape=[jax.ShapeDtypeStruct((rows, dm), F32), jax.ShapeDtypeStruct((rows, dm), F32),
                   jax.ShapeDtypeStruct((rows, dm), BF16), jax.ShapeDtypeStruct((rows, dm), BF16),
                   jax.ShapeDtypeStruct((rows, dm), BF16), jax.ShapeDtypeStruct((rows, nh), F32)],
        compiler_params=_params("parallel"),
        name="proj_odd",
    )(x, g, w_bf, gq, gk, b_f, seg)


def _post_kernel(x_ref, oa_ref, ob_ref, wo_ref, g_ref, wg_ref, wu_ref, wd_ref, y_ref):
    ha = oa_ref.shape[1]
    x1 = x_ref[...] + _dot(oa_ref[...], wo_ref[0:ha, :]) + _dot(ob_ref[...], wo_ref[ha:, :])
    h = _rms(x1, g_ref[...]).astype(BF16)
    gate = _dot(h, wg_ref[...])
    up = _dot(h, wu_ref[...])
    act = (gate * (1.0 / (1.0 + jnp.exp(-gate))) * up).astype(BF16)
    y_ref[...] = x1 + _dot(act, wd_ref[...])


def _post(x, oa, ob, oa_col, ob_col, wo, g, wg, wu, wd):
    rows, d = x.shape
    half = wo.shape[0] // 2
    tm = min(256, rows)
    return pl.pallas_call(
        _post_kernel,
        grid=(rows // tm,),
        in_specs=[pl.BlockSpec((tm, d), lambda i: (i, 0)),
                  pl.BlockSpec((tm, half), lambda i: (i, oa_col)),
                  pl.BlockSpec((tm, half), lambda i: (i, ob_col)),
                  _const_spec(wo.shape), _const_spec((1, d)), _const_spec(wg.shape),
                  _const_spec(wu.shape), _const_spec(wd.shape)],
        out_specs=pl.BlockSpec((tm, d), lambda i: (i, 0)),
        out_shape=jax.ShapeDtypeStruct((rows, d), F32),
        compiler_params=_params("parallel"),
        name="post_attn",
    )(x, oa, ob, wo, g, wg, wu, wd)


def _rel_bias_of(dist, rel_ref, h):
    d = jnp.maximum(dist, 0)
    big = jnp.full(d.shape, REL_BUCKETS // 2, jnp.int32)
    for thr in _BUCKET_THRESHOLDS:
        big = big + jnp.where(d >= thr, 1, 0)
    bucket = jnp.where(d < REL_BUCKETS // 2, d, big)
    out = jnp.full(d.shape, rel_ref[REL_BUCKETS - 1, h], F32)
    for b in range(REL_BUCKETS - 1):
        out = jnp.where(bucket == b, rel_ref[b, h], out)
    return out


def _bias_kernel(rel_ref, tiles_ref, dec_ref, *, blk, past_len):
    h = pl.program_id(0)
    row = lax.broadcasted_iota(jnp.int32, (blk, blk), 0)
    col = lax.broadcasted_iota(jnp.int32, (blk, blk), 1)
    tiles_ref[0] = _rel_bias_of(row - col, rel_ref, h)
    tiles_ref[1] = _rel_bias_of(row - col + blk, rel_ref, h)
    tiles_ref[2] = jnp.full((blk, blk), rel_ref[REL_BUCKETS - 1, h], F32)
    pos = lax.broadcasted_iota(jnp.int32, (1, past_len), 1)
    dec_ref[...] = _rel_bias_of(past_len - pos, rel_ref, h)


def _bias_tables(rel_bias, past_len):
    assert MOBA_BLOCK + 1 >= REL_MAX_DIST, "tile 2 must be all in the last bucket"
    nh = rel_bias.shape[1]
    blk = MOBA_BLOCK
    return pl.pallas_call(
        functools.partial(_bias_kernel, blk=blk, past_len=past_len),
        grid=(nh,),
        in_specs=[pl.BlockSpec(memory_space=pltpu.SMEM)],
        out_specs=[pl.BlockSpec((None, 3, blk, blk), lambda h: (h, 0, 0, 0)),
                   pl.BlockSpec((None, 1, past_len), lambda h: (h, 0, 0))],
        out_shape=[jax.ShapeDtypeStruct((nh, 3, blk, blk), F32),
                   jax.ShapeDtypeStruct((nh, 1, past_len), F32)],
        compiler_params=_params("parallel"),
        name="rel_bias_tables",
    )(rel_bias)


def _pair_queries(q2):
    lane = lax.broadcasted_iota(jnp.int32, q2.shape, 1)
    qf = q2.astype(F32)
    return [jnp.where(lane < HEAD_DIM, qf, 0.0).astype(q2.dtype), jnp.where(lane >= HEAD_DIM, qf, 0.0).astype(q2.dtype)]


def _pair_merge(a0, a1):
    lane = lax.broadcasted_iota(jnp.int32, a0.shape, 1)
    return jnp.where(lane < HEAD_DIM, a0, a1)


def _sb_kernel(q_ref, k_ref, v_ref, tri_ref, o_ref, *, tq):
    i = pl.program_id(2)
    qh = _pair_queries(q_ref[...])
    tri = tri_ref[...]
    row = lax.broadcasted_iota(jnp.int32, (tq, tq), 0)
    col = lax.broadcasted_iota(jnp.int32, (tq, tq), 1)
    strict = col < row

    def tile(j, carry, diag):
        start = pl.multiple_of(j * tq, tq)
        kt = k_ref[pl.ds(start, tq), :]
        vt = v_ref[pl.ds(start, tq), :]
        new = []
        for h in range(2):
            run, acc = carry[h]
            z = _dot_nt(qh[h], kt)
            lb, l1 = _log_sigmoid_parts(z)
            if diag:
                l1 = jnp.where(strict, l1, 0.0)
            hi, lo = _split2(l1)
            tail = _dot(hi, tri) + _dot(lo, tri)
            w = jnp.exp(lb + tail + run)
            if diag:
                w = jnp.where(strict, w, 0.0)
            acc = acc + _dot(w.astype(BF16), vt)
            run = run + tail[:, 0:1] + l1[:, 0:1]
            new.append((run, acc))
        return tuple(new)

    init = tuple((jnp.zeros((tq, 1), F32), jnp.zeros((tq, LANES), F32)) for _ in range(2))
    carry = tile(i, init, True)
    carry = lax.fori_loop(0, i, lambda t, c: tile(i - 1 - t, c, False), carry)
    o_ref[...] = _pair_merge(carry[0][1], carry[1][1]).astype(o_ref.dtype)


def _sb_attention(qb, kb, vb, tri, n_pairs, tq):
    b, t, _ = qb.shape
    return pl.pallas_call(
        functools.partial(_sb_kernel, tq=tq),
        grid=(b, n_pairs, t // tq),
        in_specs=[pl.BlockSpec((None, tq, LANES), lambda bi, p, i: (bi, i, p)),
                  pl.BlockSpec((None, t, LANES), lambda bi, p, i: (bi, 0, p)),
                  pl.BlockSpec((None, t, LANES), lambda bi, p, i: (bi, 0, p)),
                  _const_spec(tri.shape)],
        out_specs=pl.BlockSpec((None, tq, LANES), lambda bi, p, i: (bi, i, p)),
        out_shape=jax.ShapeDtypeStruct((b, t, n_pairs * LANES), BF16),
        compiler_params=_params("parallel", "parallel", "arbitrary"),
        name="sb_attention",
    )(qb, kb, vb, tri)


def _softmax_tile(s, vt, state):
    m, l, acc = state
    m_new = jnp.maximum(m, jnp.max(s, axis=1, keepdims=True))
    alpha = jnp.exp(m - m_new)
    p = jnp.exp(s - m_new)
    l = alpha * l + jnp.sum(p, axis=1, keepdims=True)
    acc = alpha * acc + _dot(p.astype(BF16), vt)
    return m_new, l, acc


def _softmax_init(tq):
    return (jnp.full((tq, 1), NEG, F32), jnp.zeros((tq, 1), F32), jnp.zeros((tq, LANES), F32))


def _softmax_finish(states, o_ref):
    outs = [acc / l for (_, l, acc) in states]
    o_ref[...] = _pair_merge(outs[0], outs[1]).astype(o_ref.dtype)


def _fox_kernel(q_ref, k_ref, v_ref, ck_ref, o_ref, *, tq):
    i = pl.program_id(2)
    qh = _pair_queries(q_ref[...])
    row = lax.broadcasted_iota(jnp.int32, (tq, tq), 0)
    col = lax.broadcasted_iota(jnp.int32, (tq, tq), 1)
    causal = col <= row

    def tile(j, states, diag):
        start = pl.multiple_of(j * tq, tq)
        kt = k_ref[pl.ds(start, tq), :]
        vt = v_ref[pl.ds(start, tq), :]
        new = []
        for h in range(2):
            s = _dot_nt(qh[h], kt) - ck_ref[h:h + 1, pl.ds(start, tq)]
            if diag:
                s = jnp.where(causal, s, NEG)
            new.append(_softmax_tile(s, vt, states[h]))
        return tuple(new)

    states = tile(i, (_softmax_init(tq), _softmax_init(tq)), True)
    states = lax.fori_loop(0, i, lambda j, st: tile(j, st, False), states)
    _softmax_finish(states, o_ref)


def _fox_attention(qb, kb, vb, ck, tq):
    b, t, dm = qb.shape
    n_pairs = dm // LANES
    return pl.pallas_call(
        functools.partial(_fox_kernel, tq=tq),
        grid=(b, n_pairs, t // tq),
        in_specs=[pl.BlockSpec((None, tq, LANES), lambda bi, p, i: (bi, i, p)),
                  pl.BlockSpec((None, t, LANES), lambda bi, p, i: (bi, 0, p)),
                  pl.BlockSpec((None, t, LANES), lambda bi, p, i: (bi, 0, p)),
                  pl.BlockSpec((None, None, 2, t), lambda bi, p, i: (bi, p, 0, 0))],
        out_specs=pl.BlockSpec((None, tq, LANES), lambda bi, p, i: (bi, i, p)),
        out_shape=jax.ShapeDtypeStruct((b, t, dm), BF16),
        compiler_params=_params("parallel", "parallel", "arbitrary"),
        name="fox_attention",
    )(qb, kb, vb, ck)


def _cumsum_kernel(x_ref, tri_ref, o_ref, *, chunk):
    tri = tri_ref[...]
    carry = jnp.zeros((x_ref.shape[0], 1), F32)
    for c in range(x_ref.shape[1] // chunk):
        a, b, d = _split3(x_ref[:, c * chunk:(c + 1) * chunk])
        cs = _dot(a, tri) + _dot(b, tri) + _dot(d, tri) + carry
        o_ref[:, c * chunk:(c + 1) * chunk] = cs
        carry = cs[:, chunk - 1:chunk]


def _cumsum_lanes(x, tri_incl):
    b, nh, t = x.shape
    return pl.pallas_call(
        functools.partial(_cumsum_kernel, chunk=tri_incl.shape[0]),
        grid=(b,),
        in_specs=[pl.BlockSpec((None, nh, t), lambda bi: (bi, 0, 0)), _const_spec(tri_incl.shape)],
        out_specs=pl.BlockSpec((None, nh, t), lambda bi: (bi, 0, 0)),
        out_shape=jax.ShapeDtypeStruct((b, nh, t), F32),
        compiler_params=_params("parallel"),
        name="logf_cumsum",
    )(x, tri_incl)


def _top_blocks(scores, n_valid, n_blocks, topk):
    blk = lax.broadcasted_iota(jnp.int32, scores.shape, 0)
    rank = jnp.zeros(scores.shape, F32)
    for m in range(n_blocks):
        sm = scores[m:m + 1, :]
        beats = (sm > scores) | ((sm == scores) & (m < blk))
        rank = rank + jnp.where(beats, jnp.where(m < n_valid, 1.0, 0.0), 0.0)
    return jnp.where((rank < topk) & (blk < n_valid), 1.0, 0.0)


def _moba_kernel(q_ref, k_ref, v_ref, qf_ref, kf_ref, bias_ref, eye_ref, o_ref, km_ref, sel_ref, *, tq, nb):
    i = pl.program_id(2)
    lane = lax.broadcasted_iota(jnp.int32, (1, LANES), 1)

    @pl.when(i == 0)
    def _():
        for n in range(nb):
            mean = jnp.sum(kf_ref[n * tq:(n + 1) * tq, :], axis=0, keepdims=True) * (1.0 / MOBA_BLOCK)
            km_ref[n:n + 1, :] = jnp.where(lane < HEAD_DIM, mean, 0.0)
            km_ref[nb + n:nb + n + 1, :] = jnp.where(lane >= HEAD_DIM, mean, 0.0)

    ka, kb2 = _split2(km_ref[...])
    qa, qb2 = _split2(qf_ref[...])
    scores = _dot_nt(ka, qa) + _dot_nt(kb2, qa) + _dot_nt(ka, qb2)
    sel = jnp.concatenate([_top_blocks(scores[h * nb:(h + 1) * nb], i, nb, MOBA_TOPK) for h in range(2)]
                          + [jnp.zeros((LANES - 2 * nb, tq), F32)], axis=0)
    sel_t = _dot_nt(eye_ref[...], sel.astype(BF16))
    for c in range(2 * nb):
        sel_ref[c] = jnp.where(sel_t[:, c:c + 1] > 0.5, 0.0, NEG)

    qh = _pair_queries(q_ref[...])
    row = lax.broadcasted_iota(jnp.int32, (tq, tq), 0)
    col = lax.broadcasted_iota(jnp.int32, (tq, tq), 1)
    causal = col <= row

    def tile(n, states, diag):
        start = pl.multiple_of(n * tq, tq)
        kt = k_ref[pl.ds(start, tq), :]
        vt = v_ref[pl.ds(start, tq), :]
        new = []
        for h in range(2):
            s = _dot_nt(qh[h], kt)
            if diag:
                s = jnp.where(causal, s + bias_ref[h, 0], NEG)
            else:
                s = s + bias_ref[h, jnp.minimum(i - n, 2)] + sel_ref[h * nb + n]
            new.append(_softmax_tile(s, vt, states[h]))
        return tuple(new)

    states = tile(i, (_softmax_init(tq), _softmax_init(tq)), True)
    states = lax.fori_loop(0, i, lambda n, st: tile(n, st, False), states)
    _softmax_finish(states, o_ref)


def _moba_attention(qb, kb, vb, qf, kf, bias_tiles, eye, n_pairs, col0):
    b, t, _ = qb.shape
    tq = MOBA_BLOCK
    nb = t // tq
    return pl.pallas_call(
        functools.partial(_moba_kernel, tq=tq, nb=nb),
        grid=(b, n_pairs, nb),
        in_specs=[pl.BlockSpec((None, tq, LANES), lambda bi, p, i: (bi, i, col0 + p)),
                  pl.BlockSpec((None, t, LANES), lambda bi, p, i: (bi, 0, col0 + p)),
                  pl.BlockSpec((None, t, LANES), lambda bi, p, i: (bi, 0, col0 + p)),
                  pl.BlockSpec((None, tq, LANES), lambda bi, p, i: (bi, i, p)),
                  pl.BlockSpec((None, t, LANES), lambda bi, p, i: (bi, 0, col0 + p)),
                  pl.BlockSpec((2, 3, tq, tq), lambda bi, p, i: (p, 0, 0, 0)),
                  _const_spec(eye.shape)],
        out_specs=pl.BlockSpec((None, tq, LANES), lambda bi, p, i: (bi, i, p)),
        out_shape=jax.ShapeDtypeStruct((b, t, n_pairs * LANES), BF16),
        scratch_shapes=[pltpu.VMEM((2 * nb, LANES), F32), pltpu.VMEM((2 * nb, tq, 1), F32)],
        compiler_params=_params("parallel", "parallel", "arbitrary"),
        name="moba_attention",
    )(qb, kb, vb, qf, kf, bias_tiles, eye)


def _page_copies(pt_ref, b, slot, n_pages, pools_bufs_sems):
    cps = []
    for pool, buf, sem in pools_bufs_sems:
        for p in range(n_pages):
            cps.append(pltpu.make_async_copy(pool.at[pt_ref[b, p]],
                                             buf.at[slot, pl.ds(p * PAGE_SIZE, PAGE_SIZE)],
                                             sem.at[slot, p]))
    return cps


def _gather_pipeline(pt_ref, n_pages, pools_bufs_sems):
    b = pl.program_id(0)
    slot = b % 2

    @pl.when(b == 0)
    def _():
        for cp in _page_copies(pt_ref, b, slot, n_pages, pools_bufs_sems):
            cp.start()

    @pl.when(b + 1 < pl.num_programs(0))
    def _():
        for cp in _page_copies(pt_ref, b + 1, 1 - slot, n_pages, pools_bufs_sems):
            cp.start()

    for cp in _page_copies(pt_ref, b, slot, n_pages, pools_bufs_sems):
        cp.wait()
    return slot


def _head_rows(x_row, n_heads):
    shape = (n_heads, x_row.shape[1])
    lane = lax.broadcasted_iota(jnp.int32, shape, 1)
    row = lax.broadcasted_iota(jnp.int32, shape, 0)
    own = jnp.right_shift(lane, HEAD_SHIFT) == row
    rows = jnp.where(own, jnp.broadcast_to(x_row.astype(F32), shape), 0.0)
    return rows.astype(x_row.dtype), own


def _dec_even_kernel(pt_ref, qb_ref, qf_ref, kn_ref, vn_ref, dbias_ref, b0_ref, tri_ref, kc_ref, vc_ref,
                     o_ref, kbuf, vbuf, ksem, vsem, *, n_pages, n_sb, chunk):
    slot = _gather_pipeline(pt_ref, n_pages, [(kc_ref, kbuf, ksem), (vc_ref, vbuf, vsem)])
    past = n_pages * PAGE_SIZE
    nc = past // chunk
    nh = qb_ref.shape[1] // HEAD_DIM
    n_mb = nh - n_sb
    half = n_sb * HEAD_DIM

    q16, own = _head_rows(qb_ref[...], nh)
    zs = [_dot_nt(q16, kbuf[slot, c * chunk:(c + 1) * chunk, :].astype(BF16)) for c in range(nc)]

    tri = tri_ref[...]
    run = jnp.zeros((n_sb, 1), F32)
    w_sb = [None] * nc
    for c in reversed(range(nc)):
        lb, l1 = _log_sigmoid_parts(zs[c][0:n_sb])
        hi, lo = _split2(l1)
        tail = _dot(hi, tri) + _dot(lo, tri)
        w_sb[c] = jnp.exp(lb + tail + run)
        run = run + tail[:, 0:1] + l1[:, 0:1]

    qf_rows, _ = _head_rows(qf_ref[...], n_mb)
    sc = []
    for n in range(nc):
        mean = jnp.sum(kbuf[slot, n * chunk:(n + 1) * chunk, half:], axis=0, keepdims=True) * (1.0 / MOBA_BLOCK)
        sc.append(jnp.sum(qf_rows * mean, axis=1, keepdims=True))
    lg = []
    for n in range(nc):
        rank = jnp.zeros((n_mb, 1), F32)
        for m in range(nc):
            if m != n:
                beats = (sc[m] > sc[n]) | ((sc[m] == sc[n]) & (m < n))
                rank = rank + jnp.where(beats, 1.0, 0.0)
        mask = jnp.where(rank < MOBA_TOPK, 0.0, NEG)
        lg.append(zs[n][n_sb:] + dbias_ref[:, n * chunk:(n + 1) * chunk] + mask)
    kn = kn_ref[...].astype(BF16).astype(F32)
    z_new = jnp.sum(q16.astype(F32) * kn, axis=1, keepdims=True)
    lg_new = z_new[n_sb:] + b0_ref[...]
    m = lg_new
    for n in range(nc):
        m = jnp.maximum(m, jnp.max(lg[n], axis=1, keepdims=True))
    p_new = jnp.exp(lg_new - m)
    l = p_new
    acc = jnp.zeros((nh, nh * HEAD_DIM), F32)
    for c in range(nc):
        p = jnp.exp(lg[c] - m)
        l = l + jnp.sum(p, axis=1, keepdims=True)
        w = jnp.concatenate([w_sb[c], p], axis=0).astype(BF16)
        acc = acc + _dot(w, vbuf[slot, c * chunk:(c + 1) * chunk, :].astype(BF16))
    p_new_all = jnp.concatenate([jnp.zeros((n_sb, 1), F32), p_new], axis=0)
    inv_l = jnp.concatenate([jnp.ones((n_sb, 1), F32), 1.0 / l], axis=0)
    out = (acc + p_new_all * vn_ref[...]) * inv_l
    o_ref[...] = jnp.sum(jnp.where(own, out, 0.0), axis=0, keepdims=True).astype(o_ref.dtype)


def _dec_even(page_table, qb, qf, kn, vn, dbias, b0, tri, k_pool, v_pool, n_sb):
    nseq, n_pages = page_table.shape
    dm = qb.shape[-1]
    past = n_pages * PAGE_SIZE
    seq_row = lambda w: pl.BlockSpec((None, 1, w), lambda b, pt: (b, 0, 0))
    const = lambda shape: pl.BlockSpec(shape, lambda b, pt: (0,) * len(shape), pipeline_mode=pl.Buffered(1))
    grid_spec = pltpu.PrefetchScalarGridSpec(
        num_scalar_prefetch=1,
        grid=(nseq,),
        in_specs=[seq_row(dm), seq_row(qf.shape[-1]), seq_row(dm), seq_row(dm),
                  const(dbias.shape), const(b0.shape), const(tri.shape),
                  pl.BlockSpec(memory_space=pl.ANY), pl.BlockSpec(memory_space=pl.ANY)],
        out_specs=seq_row(dm),
        scratch_shapes=[pltpu.VMEM((2, past, dm), F32), pltpu.VMEM((2, past, dm), F32),
                        pltpu.SemaphoreType.DMA((2, n_pages)), pltpu.SemaphoreType.DMA((2, n_pages))],
    )
    return pl.pallas_call(
        functools.partial(_dec_even_kernel, n_pages=n_pages, n_sb=n_sb, chunk=MOBA_BLOCK),
        grid_spec=grid_spec,
        out_shape=jax.ShapeDtypeStruct((nseq, 1, dm), BF16),
        compiler_params=_params("arbitrary"),
        name="decode_even",
    )(page_table, qb, qf, kn, vn, dbias, b0, tri, k_pool, v_pool)


def _dec_odd_kernel(pt_ref, qb_ref, kn_ref, vn_ref, lfn_ref, tri_ref, eye_ref, kc_ref, vc_ref, lc_ref,
                    o_ref, kbuf, vbuf, lbuf, ksem, vsem, lsem, *, n_pages, chunk):
    slot = _gather_pipeline(pt_ref, n_pages, [(kc_ref, kbuf, ksem), (vc_ref, vbuf, vsem), (lc_ref, lbuf, lsem)])
    past = n_pages * PAGE_SIZE
    nc = past // chunk
    nh = qb_ref.shape[1] // HEAD_DIM

    q16, own = _head_rows(qb_ref[...], nh)
    tri = tri_ref[...]
    eye = eye_ref[...]
    run = lfn_ref[...]
    lg = [None] * nc
    for c in reversed(range(nc)):
        lf_t = [_dot_nt(eye, piece) for piece in _split3(lbuf[slot, c * chunk:(c + 1) * chunk, :])]
        lf_c = lf_t[0] + lf_t[1] + lf_t[2]
        a, b, d = _split3(lf_c)
        tail = _dot(a, tri) + _dot(b, tri) + _dot(d, tri)
        z = _dot_nt(q16, kbuf[slot, c * chunk:(c + 1) * chunk, :].astype(BF16))
        lg[c] = z + tail + run
        run = run + tail[:, 0:1] + lf_c[:, 0:1]
    kn = kn_ref[...].astype(BF16).astype(F32)
    lg_new = jnp.sum(q16.astype(F32) * kn, axis=1, keepdims=True)
    m = lg_new
    for c in range(nc):
        m = jnp.maximum(m, jnp.max(lg[c], axis=1, keepdims=True))
    p_new = jnp.exp(lg_new - m)
    l = p_new
    acc = jnp.zeros((nh, nh * HEAD_DIM), F32)
    for c in range(nc):
        p = jnp.exp(lg[c] - m)
        l = l + jnp.sum(p, axis=1, keepdims=True)
        acc = acc + _dot(p.astype(BF16), vbuf[slot, c * chunk:(c + 1) * chunk, :].astype(BF16))
    out = (acc + p_new * vn_ref[...]) / l
    o_ref[...] = jnp.sum(jnp.where(own, out, 0.0), axis=0, keepdims=True).astype(o_ref.dtype)


def _dec_odd(page_table, qb, kn, vn, lfn, tri, eye, k_pool, v_pool, lf_pool):
    nseq, n_pages = page_table.shape
    dm = qb.shape[-1]
    nh = lfn.shape[1]
    past = n_pages * PAGE_SIZE
    seq_row = lambda w: pl.BlockSpec((None, 1, w), lambda b, pt: (b, 0, 0))
    const = lambda shape: pl.BlockSpec(shape, lambda b, pt: (0,) * len(shape), pipeline_mode=pl.Buffered(1))
    grid_spec = pltpu.PrefetchScalarGridSpec(
        num_scalar_prefetch=1,
        grid=(nseq,),
        in_specs=[seq_row(dm), seq_row(dm), seq_row(dm), pl.BlockSpec((None, nh, 1), lambda b, pt: (b, 0, 0)),
                  const(tri.shape), const(eye.shape),
                  pl.BlockSpec(memory_space=pl.ANY), pl.BlockSpec(memory_space=pl.ANY),
                  pl.BlockSpec(memory_space=pl.ANY)],
        out_specs=seq_row(dm),
        scratch_shapes=[pltpu.VMEM((2, past, dm), F32), pltpu.VMEM((2, past, dm), F32),
                        pltpu.VMEM((2, past, nh), F32),
                        pltpu.SemaphoreType.DMA((2, n_pages)), pltpu.SemaphoreType.DMA((2, n_pages)),
                        pltpu.SemaphoreType.DMA((2, n_pages))],
    )
    return pl.pallas_call(
        functools.partial(_dec_odd_kernel, n_pages=n_pages, chunk=MOBA_BLOCK),
        grid_spec=grid_spec,
        out_shape=jax.ShapeDtypeStruct((nseq, 1, dm), BF16),
        compiler_params=_params("arbitrary"),
        name="decode_odd",
    )(page_table, qb, kn, vn, lfn, tri, eye, k_pool, v_pool, lf_pool)


def _tri(n, keep):
    r = lax.broadcasted_iota(jnp.int32, (n, n), 0)
    c = lax.broadcasted_iota(jnp.int32, (n, n), 1)
    return keep(r, c).astype(BF16)


def kernel(x_prompt, x_sample, cache_k_even, cache_v_even, cache_k_odd, cache_v_odd, cache_logf_odd, page_table, attn_norm, ffn_norm, w_in_even, w_out_even, q_norm_even, k_norm_even, rel_bias, w_in_odd, b_f_odd, w_out_odd, q_norm_odd, k_norm_odd, w_gate, w_up, w_down):
    bsz, seq, d = x_prompt.shape
    nseq, dec_seq, _ = x_sample.shape
    depth = attn_norm.shape[0]
    n_pool, page = cache_k_even.shape[1:3]
    nh = cache_k_even.shape[3]
    dm = nh * HEAD_DIM
    n_mb = rel_bias.shape[1]
    n_sb = nh - n_mb
    half = n_sb * HEAD_DIM
    n_pages = page_table.shape[1]
    past = n_pages * page
    assert dec_seq == 1 and page == PAGE_SIZE and n_sb == n_mb
    assert seq % MOBA_BLOCK == 0 and past % MOBA_BLOCK == 0 and half % MXU_DIM == 0

    tri_after = _tri(MOBA_BLOCK, lambda r, c: r > c)
    tri_incl = _tri(MOBA_BLOCK, lambda r, c: r <= c)
    eye_blk = _tri(MOBA_BLOCK, lambda r, c: r == c)
    eye_h = _tri(nh, lambda r, c: r == c)
    seg = _tri(MXU_DIM, lambda r, c: (r // HEAD_DIM) == (c // HEAD_DIM)) * (1.0 / HEAD_DIM)

    xp = x_prompt.reshape(bsz * seq, d)
    xs = x_sample.reshape(nseq, d)
    bias_tiles, dec_bias = _bias_tables(rel_bias, past)
    dec_bias = dec_bias.reshape(n_mb, past)
    b0 = rel_bias[0].reshape(n_mb, 1)

    outs = {name: [] for name in ("ek_p", "ev_p", "ek_s", "ev_s", "ok_p", "ov_p", "of_p", "ok_s", "ov_s", "of_s")}
    for l in range(depth):
        i = l // 2
        g_attn = attn_norm[l].reshape(1, d)
        if l % 2 == 0:
            w_in = w_in_even[i].astype(BF16)
            gq = jnp.tile(q_norm_even[i], n_mb).reshape(1, half)
            gk = jnp.tile(k_norm_even[i], n_mb).reshape(1, half)
            k_p, v_p, qb, kb, vb, qf = _proj_even(xp, g_attn, w_in, gq, gk, seg)
            r3 = lambda a: a.reshape(bsz, seq, a.shape[-1])
            o_sb = _sb_attention(r3(qb), r3(kb), r3(vb), tri_after, n_sb // 2, MOBA_BLOCK)
            o_mb = _moba_attention(r3(qb), r3(kb), r3(vb), r3(qf), r3(k_p), bias_tiles, eye_blk,
                                   n_mb // 2, half // LANES)
            outs["ek_p"].append(k_p.reshape(bsz, seq, nh, HEAD_DIM))
            outs["ev_p"].append(v_p.reshape(bsz, seq, nh, HEAD_DIM))
            k_s, v_s, qb_s, _, _, qf_s = _proj_even(xs, g_attn, w_in, gq, gk, seg)
            s3 = lambda a: a.reshape(nseq, 1, a.shape[-1])
            o_s = _dec_even(page_table, s3(qb_s), s3(qf_s), s3(k_s), s3(v_s), dec_bias, b0, tri_after,
                            cache_k_even[i].reshape(n_pool, page, dm), cache_v_even[i].reshape(n_pool, page, dm),
                            n_sb).reshape(nseq, dm)
            outs["ek_s"].append(k_s.reshape(nseq, 1, nh, HEAD_DIM))
            outs["ev_s"].append(v_s.reshape(nseq, 1, nh, HEAD_DIM))
            w_out = w_out_even[i].astype(BF16)
            o_p = (o_sb.reshape(bsz * seq, half), o_mb.reshape(bsz * seq, half), 0, 0)
        else:
            w_in = jnp.pad(w_in_odd[i], ((0, 0), (0, LANES - nh))).astype(BF16)
            b_f = jnp.pad(b_f_odd[i], (0, LANES - nh)).reshape(1, LANES)
            gq = jnp.tile(q_norm_odd[i], nh).reshape(1, dm)
            gk = jnp.tile(k_norm_odd[i], nh).reshape(1, dm)
            k_p, v_p, qb, kb, vb, lf_p = _proj_odd(xp, g_attn, w_in, gq, gk, b_f, seg, nh)
            r3 = lambda a: a.reshape(bsz, seq, a.shape[-1])
            c_t = _cumsum_lanes(jnp.swapaxes(r3(lf_p), 1, 2), tri_incl)
            o = _fox_attention(r3(qb), r3(kb), r3(vb), c_t.reshape(bsz, nh // 2, 2, seq), MOBA_BLOCK)
            outs["ok_p"].append(k_p.reshape(bsz, seq, nh, HEAD_DIM))
            outs["ov_p"].append(v_p.reshape(bsz, seq, nh, HEAD_DIM))
            outs["of_p"].append(lf_p.reshape(bsz, seq, nh))
            k_s, v_s, qb_s, _, _, lf_s = _proj_odd(xs, g_attn, w_in, gq, gk, b_f, seg, nh)
            s3 = lambda a: a.reshape(nseq, 1, a.shape[-1])
            o_s = _dec_odd(page_table, s3(qb_s), s3(k_s), s3(v_s), lf_s.reshape(nseq, nh, 1), tri_after, eye_h,
                           cache_k_odd[i].reshape(n_pool, page, dm), cache_v_odd[i].reshape(n_pool, page, dm),
                           cache_logf_odd[i]).reshape(nseq, dm)
            outs["ok_s"].append(k_s.reshape(nseq, 1, nh, HEAD_DIM))
            outs["ov_s"].append(v_s.reshape(nseq, 1, nh, HEAD_DIM))
            outs["of_s"].append(lf_s.reshape(nseq, 1, nh))
            w_out = w_out_odd[i].astype(BF16)
            o2 = o.reshape(bsz * seq, dm)
            o_p = (o2, o2, 0, 1)
        g_ffn = ffn_norm[l].reshape(1, d)
        wg, wu, wd = w_gate[l].astype(BF16), w_up[l].astype(BF16), w_down[l].astype(BF16)
        xp = _post(xp, *o_p, w_out, g_ffn, wg, wu, wd)
        xs = _post(xs, o_s, o_s, 0, 1, w_out, g_ffn, wg, wu, wd)

    st = lambda name: jnp.stack(outs[name])
    return (xp.reshape(bsz, seq, d), xs.reshape(nseq, 1, d),
            st("ek_p"), st("ev_p"), st("ek_s"), st("ev_s"),
            st("ok_p"), st("ov_p"), st("of_p"), st("ok_s"), st("ov_s"), st("of_s"))
```

```python
import functools
import math

import jax
import jax.numpy as jnp
from jax import lax
from jax.experimental import pallas as pl
from jax.experimental.pallas import tpu as pltpu

F32 = jnp.float32
BF16 = jnp.bfloat16

HEAD_DIM = 64
EPS = 1e-6
SCALE = HEAD_DIM ** -0.5
MOBA_BLOCK = 256
MOBA_TOPK = 3
PAGE_SIZE = 128
REL_BUCKETS = 32
REL_MAX_DIST = 128
NEG = -1e30
LANES = 128
MXU_DIM = 256
VMEM_LIMIT = 56 * 1024 * 1024
ROW_TILE = 256
SB_TILE = 512
FOX_TILE = 512
PV_HEAD_GROUP = 4

NT = (((1,), (1,)), ((), ()))


def _bucket_thresholds():
    max_exact = REL_BUCKETS // 2
    def bucket(d):
        return min(max_exact + int(math.log(d / max_exact) / math.log(REL_MAX_DIST / max_exact)
                                   * (REL_BUCKETS - max_exact)), REL_BUCKETS - 1)
    return [min(d for d in range(max_exact, REL_MAX_DIST + 1) if bucket(d) >= b)
            for b in range(max_exact + 1, REL_BUCKETS)]


_BUCKET_THRESHOLDS = _bucket_thresholds()


def _params(*sem):
    return pltpu.CompilerParams(dimension_semantics=sem, vmem_limit_bytes=VMEM_LIMIT)


def _const_spec(shape):
    return pl.BlockSpec(shape, lambda *_: (0,) * len(shape), pipeline_mode=pl.Buffered(1))


def _log_sigmoid_parts(z):
    sp = jnp.log1p(jnp.exp(-jnp.abs(z)))
    return jnp.minimum(z, 0.0) - sp, jnp.minimum(-z, 0.0) - sp


def _split2(x):
    hi = x.astype(BF16)
    lo = (x - hi.astype(F32)).astype(BF16)
    return hi, lo


def _split3(x):
    a = x.astype(BF16)
    r = x - a.astype(F32)
    b = r.astype(BF16)
    c = (r - b.astype(F32)).astype(BF16)
    return a, b, c


def _dot(a, b):
    return jnp.dot(a, b, preferred_element_type=F32)


def _dot_nt(a, b):
    return lax.dot_general(a, b, NT, preferred_element_type=F32)


def _rms(x, g):
    ms = jnp.mean(x * x, axis=-1, keepdims=True)
    return x * lax.rsqrt(ms + EPS) * g


def _head_rms(t, gain, seg):
    outs = []
    for c in range(t.shape[1] // MXU_DIM):
        tc = t[:, c * MXU_DIM:(c + 1) * MXU_DIM]
        hi, lo = _split2(tc * tc)
        ms = _dot(hi, seg) + _dot(lo, seg)
        outs.append(tc * lax.rsqrt(ms + EPS) * gain[:, c * MXU_DIM:(c + 1) * MXU_DIM])
    return jnp.concatenate(outs, axis=1)


def _head_rms_t(t, gain, seg):
    outs = []
    for c in range(t.shape[0] // MXU_DIM):
        tc = t[c * MXU_DIM:(c + 1) * MXU_DIM]
        hi, lo = _split2(tc * tc)
        ms = _dot(seg, hi) + _dot(seg, lo)
        outs.append(tc * lax.rsqrt(ms + EPS) * gain[c * MXU_DIM:(c + 1) * MXU_DIM])
    return jnp.concatenate(outs, axis=0)


def _proj_even_kernel(x_ref, g_ref, wq_ref, wkv_ref, gq_ref, gk_ref, seg_ref,
                      kt_ref, vt_ref, kbt_ref, vbt_ref, qb_ref, qf_ref, *, half):
    h = _rms(x_ref[...], g_ref[...]).astype(BF16)
    seg = seg_ref[...]
    uq = _dot(h, wq_ref[...])
    q = jnp.concatenate([uq[:, 0:half], _head_rms(uq[:, half:], gq_ref[...], seg)], axis=1) * SCALE
    qf_ref[...] = q
    qb_ref[...] = q.astype(BF16)
    ukv = _dot_nt(wkv_ref[...], h)
    k = jnp.concatenate([ukv[0:half], _head_rms_t(ukv[half:2 * half], gk_ref[...], seg)], axis=0)
    v = ukv[2 * half:]
    kt_ref[...] = k
    vt_ref[...] = v
    kbt_ref[...] = k.astype(BF16)
    vbt_ref[...] = v.astype(BF16)


def _feature_major_specs(groups, seq, tm, feats):
    nt = seq // tm
    spec = pl.BlockSpec((None, feats, tm), lambda i: (i // nt, 0, i % nt))
    return spec, lambda dt: jax.ShapeDtypeStruct((groups, feats, seq), dt)


def _proj_even(x, g, wq, wkv, gq, gk_col, seg, groups):
    rows, d = x.shape
    dm = wq.shape[1]
    half = dm // 2
    seq = rows // groups
    tm = min(ROW_TILE, seq)
    row = lambda w: pl.BlockSpec((tm, w), lambda i: (i, 0))
    fm, fm_shape = _feature_major_specs(groups, seq, tm, dm)
    return pl.pallas_call(
        functools.partial(_proj_even_kernel, half=half),
        grid=(rows // tm,),
        in_specs=[row(d), _const_spec((1, d)), _const_spec(wq.shape), _const_spec(wkv.shape),
                  _const_spec((1, half)), _const_spec((half, 1)), _const_spec(seg.shape)],
        out_specs=[fm, fm, fm, fm, row(dm), row(dm)],
        out_shape=[fm_shape(F32), fm_shape(F32), fm_shape(BF16), fm_shape(BF16),
                   jax.ShapeDtypeStruct((rows, dm), BF16), jax.ShapeDtypeStruct((rows, dm), F32)],
        compiler_params=_params("parallel"),
        name="proj_even",
    )(x, g, wq, wkv, gq, gk_col, seg)


def _proj_odd_kernel(x_ref, g_ref, wq_ref, wkv_ref, gq_ref, gk_ref, bf_ref, seg_ref,
                     kt_ref, vt_ref, kbt_ref, vbt_ref, qb_ref, qf_ref, lft_ref, *, dm, nh):
    h = _rms(x_ref[...], g_ref[...]).astype(BF16)
    seg = seg_ref[...]
    q = _head_rms(_dot(h, wq_ref[...]), gq_ref[...], seg) * SCALE
    qf_ref[...] = q
    qb_ref[...] = q.astype(BF16)
    ukv = _dot_nt(wkv_ref[...], h)
    k = _head_rms_t(ukv[0:dm], gk_ref[...], seg)
    v = ukv[dm:2 * dm]
    kt_ref[...] = k
    vt_ref[...] = v
    kbt_ref[...] = k.astype(BF16)
    vbt_ref[...] = v.astype(BF16)
    lf, _ = _log_sigmoid_parts(ukv[2 * dm:] + bf_ref[...])
    lft_ref[...] = lf[0:nh]


def _proj_odd(x, g, wq, wkv, gq, gk_col, bf_col, seg, nh, groups):
    rows, d = x.shape
    dm = wq.shape[1]
    seq = rows // groups
    tm = min(ROW_TILE, seq)
    row = lambda w: pl.BlockSpec((tm, w), lambda i: (i, 0))
    fm, fm_shape = _feature_major_specs(groups, seq, tm, dm)
    lfm, lfm_shape = _feature_major_specs(groups, seq, tm, nh)
    return pl.pallas_call(
        functools.partial(_proj_odd_kernel, dm=dm, nh=nh),
        grid=(rows // tm,),
        in_specs=[row(d), _const_spec((1, d)), _const_spec(wq.shape), _const_spec(wkv.shape),
                  _const_spec((1, dm)), _const_spec((dm, 1)), _const_spec((LANES, 1)), _const_spec(seg.shape)],
        out_specs=[fm, fm, fm, fm, row(dm), row(dm), lfm],
        out_shape=[fm_shape(F32), fm_shape(F32), fm_shape(BF16), fm_shape(BF16),
                   jax.ShapeDtypeStruct((rows, dm), BF16), jax.ShapeDtypeStruct((rows, dm), F32), lfm_shape(F32)],
        compiler_params=_params("parallel"),
        name="proj_odd",
    )(x, g, wq, wkv, gq, gk_col, bf_col, seg)


def _post_kernel(x_ref, oa_ref, ob_ref, wo_ref, g_ref, wg_ref, wu_ref, wd_ref, y_ref):
    ha = oa_ref.shape[1]
    x1 = (x_ref[...] + _dot(oa_ref[...].astype(BF16), wo_ref[0:ha, :])
          + _dot(ob_ref[...].astype(BF16), wo_ref[ha:, :]))
    h = _rms(x1, g_ref[...]).astype(BF16)
    gate = _dot(h, wg_ref[...])
    up = _dot(h, wu_ref[...])
    act = (gate * (1.0 / (1.0 + jnp.exp(-gate))) * up).astype(BF16)
    y_ref[...] = x1 + _dot(act, wd_ref[...])


def _post(x, oa, ob, oa_col, ob_col, wo, g, wg, wu, wd):
    rows, d = x.shape
    half = wo.shape[0] // 2
    tm = min(ROW_TILE, rows)
    return pl.pallas_call(
        _post_kernel,
        grid=(rows // tm,),
        in_specs=[pl.BlockSpec((tm, d), lambda i: (i, 0)),
                  pl.BlockSpec((tm, half), lambda i: (i, oa_col)),
                  pl.BlockSpec((tm, half), lambda i: (i, ob_col)),
                  _const_spec(wo.shape), _const_spec((1, d)), _const_spec(wg.shape),
                  _const_spec(wu.shape), _const_spec(wd.shape)],
        out_specs=pl.BlockSpec((tm, d), lambda i: (i, 0)),
        out_shape=jax.ShapeDtypeStruct((rows, d), F32),
        compiler_params=_params("parallel"),
        name="post_attn",
    )(x, oa, ob, wo, g, wg, wu, wd)


def _rel_bias_of(dist, rel_ref, h):
    d = jnp.maximum(dist, 0)
    big = jnp.full(d.shape, REL_BUCKETS // 2, jnp.int32)
    for thr in _BUCKET_THRESHOLDS:
        big = big + jnp.where(d >= thr, 1, 0)
    bucket = jnp.where(d < REL_BUCKETS // 2, d, big)
    out = jnp.full(d.shape, rel_ref[REL_BUCKETS - 1, h], F32)
    for b in range(REL_BUCKETS - 1):
        out = jnp.where(bucket == b, rel_ref[b, h], out)
    return out


def _bias_kernel(rel_ref, tiles_ref, dec_ref, *, blk, past_len):
    h = pl.program_id(0)
    row = lax.broadcasted_iota(jnp.int32, (blk, blk), 0)
    col = lax.broadcasted_iota(jnp.int32, (blk, blk), 1)
    tiles_ref[0] = _rel_bias_of(row - col, rel_ref, h)
    tiles_ref[1] = _rel_bias_of(row - col + blk, rel_ref, h)
    tiles_ref[2] = jnp.full((blk, blk), rel_ref[REL_BUCKETS - 1, h], F32)
    pos = lax.broadcasted_iota(jnp.int32, (1, past_len), 1)
    dec_ref[...] = _rel_bias_of(past_len - pos, rel_ref, h)


def _bias_tables(rel_bias, past_len):
    assert MOBA_BLOCK + 1 >= REL_MAX_DIST, "tile 2 must be all in the last bucket"
    nh = rel_bias.shape[1]
    blk = MOBA_BLOCK
    return pl.pallas_call(
        functools.partial(_bias_kernel, blk=blk, past_len=past_len),
        grid=(nh,),
        in_specs=[pl.BlockSpec(memory_space=pltpu.SMEM)],
        out_specs=[pl.BlockSpec((None, 3, blk, blk), lambda h: (h, 0, 0, 0)),
                   pl.BlockSpec((None, 1, past_len), lambda h: (h, 0, 0))],
        out_shape=[jax.ShapeDtypeStruct((nh, 3, blk, blk), F32),
                   jax.ShapeDtypeStruct((nh, 1, past_len), F32)],
        compiler_params=_params("parallel"),
        name="rel_bias_tables",
    )(rel_bias)


def _pair_queries(q2):
    lane = lax.broadcasted_iota(jnp.int32, q2.shape, 1)
    qf = q2.astype(F32)
    return [jnp.where(lane < HEAD_DIM, qf, 0.0).astype(q2.dtype), jnp.where(lane >= HEAD_DIM, qf, 0.0).astype(q2.dtype)]


def _pair_merge(a0, a1):
    lane = lax.broadcasted_iota(jnp.int32, a0.shape, 1)
    return jnp.where(lane < HEAD_DIM, a0, a1)


def _kv_tile(k_ref, v_ref, j, tk):
    start = pl.multiple_of(j * tk, tk)
    return k_ref[:, pl.ds(start, tk)], v_ref[:, pl.ds(start, tk)], start


def _sb_kernel(q_ref, k_ref, v_ref, tri_ref, o_ref, *, tq):
    i = pl.program_id(2)
    qh = _pair_queries(q_ref[...])
    tri = tri_ref[...]
    row = lax.broadcasted_iota(jnp.int32, (tq, tq), 0)
    col = lax.broadcasted_iota(jnp.int32, (tq, tq), 1)
    strict = col < row

    def tile(j, carry, diag):
        kt, vt, _ = _kv_tile(k_ref, v_ref, j, tq)
        new = []
        for h in range(2):
            run, acc = carry[h]
            z = _dot(qh[h], kt)
            lb, l1 = _log_sigmoid_parts(z)
            if diag:
                l1 = jnp.where(strict, l1, 0.0)
            hi, lo = _split2(l1)
            tail = _dot(hi, tri) + _dot(lo, tri)
            w = jnp.exp(lb + tail + run)
            if diag:
                w = jnp.where(strict, w, 0.0)
            acc = acc + _dot_nt(w.astype(BF16), vt)
            run = run + tail[:, 0:1] + l1[:, 0:1]
            new.append((run, acc))
        return tuple(new)

    init = tuple((jnp.zeros((tq, 1), F32), jnp.zeros((tq, LANES), F32)) for _ in range(2))
    carry = tile(i, init, True)
    carry = lax.fori_loop(0, i, lambda t, c: tile(i - 1 - t, c, False), carry)
    o_ref[...] = _pair_merge(carry[0][1], carry[1][1]).astype(o_ref.dtype)


def _pair_specs(t, tq, col0):
    q = pl.BlockSpec((None, tq, LANES), lambda bi, p, i: (bi, i, col0 + p))
    kv = pl.BlockSpec((None, LANES, t), lambda bi, p, i: (bi, col0 + p, 0))
    o = pl.BlockSpec((None, tq, LANES), lambda bi, p, i: (bi, i, p))
    return q, kv, o


def _sb_attention(qb, kbt, vbt, tri, n_pairs, tq):
    b, t, _ = qb.shape
    q_spec, kv_spec, o_spec = _pair_specs(t, tq, 0)
    return pl.pallas_call(
        functools.partial(_sb_kernel, tq=tq),
        grid=(b, n_pairs, t // tq),
        in_specs=[q_spec, kv_spec, kv_spec, _const_spec(tri.shape)],
        out_specs=o_spec,
        out_shape=jax.ShapeDtypeStruct((b, t, n_pairs * LANES), BF16),
        compiler_params=_params("parallel", "parallel", "arbitrary"),
        name="sb_attention",
    )(qb, kbt, vbt, tri)


def _softmax_tile(s, vt, state):
    m, l, acc = state
    m_new = jnp.maximum(m, jnp.max(s, axis=1, keepdims=True))
    alpha = jnp.exp(m - m_new)
    p = jnp.exp(s - m_new)
    l = alpha * l + jnp.sum(p, axis=1, keepdims=True)
    acc = alpha * acc + _dot_nt(p.astype(BF16), vt)
    return m_new, l, acc


def _softmax_init(tq):
    return (jnp.full((tq, 1), NEG, F32), jnp.zeros((tq, 1), F32), jnp.zeros((tq, LANES), F32))


def _softmax_finish(states, o_ref):
    outs = [acc / l for (_, l, acc) in states]
    o_ref[...] = _pair_merge(outs[0], outs[1]).astype(o_ref.dtype)


def _fox_kernel(q_ref, k_ref, v_ref, ck_ref, o_ref, *, tq):
    i = pl.program_id(2)
    qh = _pair_queries(q_ref[...])
    row = lax.broadcasted_iota(jnp.int32, (tq, tq), 0)
    col = lax.broadcasted_iota(jnp.int32, (tq, tq), 1)
    causal = col <= row

    def tile(j, states, diag):
        kt, vt, start = _kv_tile(k_ref, v_ref, j, tq)
        new = []
        for h in range(2):
            s = _dot(qh[h], kt) - ck_ref[h:h + 1, pl.ds(start, tq)]
            if diag:
                s = jnp.where(causal, s, NEG)
            new.append(_softmax_tile(s, vt, states[h]))
        return tuple(new)

    states = tile(i, (_softmax_init(tq), _softmax_init(tq)), True)
    states = lax.fori_loop(0, i, lambda j, st: tile(j, st, False), states)
    _softmax_finish(states, o_ref)


def _fox_attention(qb, kbt, vbt, ck, tq):
    b, t, dm = qb.shape
    n_pairs = dm // LANES
    q_spec, kv_spec, o_spec = _pair_specs(t, tq, 0)
    return pl.pallas_call(
        functools.partial(_fox_kernel, tq=tq),
        grid=(b, n_pairs, t // tq),
        in_specs=[q_spec, kv_spec, kv_spec,
                  pl.BlockSpec((None, None, 2, t), lambda bi, p, i: (bi, p, 0, 0))],
        out_specs=o_spec,
        out_shape=jax.ShapeDtypeStruct((b, t, dm), BF16),
        compiler_params=_params("parallel", "parallel", "arbitrary"),
        name="fox_attention",
    )(qb, kbt, vbt, ck)


def _cumsum_kernel(x_ref, tri_ref, o_ref, *, chunk):
    tri = tri_ref[...]
    carry = jnp.zeros((x_ref.shape[0], 1), F32)
    for c in range(x_ref.shape[1] // chunk):
        a, b, d = _split3(x_ref[:, c * chunk:(c + 1) * chunk])
        cs = _dot(a, tri) + _dot(b, tri) + _dot(d, tri) + carry
        o_ref[:, c * chunk:(c + 1) * chunk] = cs
        carry = cs[:, chunk - 1:chunk]


def _cumsum_lanes(x, tri_incl):
    b, nh, t = x.shape
    return pl.pallas_call(
        functools.partial(_cumsum_kernel, chunk=tri_incl.shape[0]),
        grid=(b,),
        in_specs=[pl.BlockSpec((None, nh, t), lambda bi: (bi, 0, 0)), _const_spec(tri_incl.shape)],
        out_specs=pl.BlockSpec((None, nh, t), lambda bi: (bi, 0, 0)),
        out_shape=jax.ShapeDtypeStruct((b, nh, t), F32),
        compiler_params=_params("parallel"),
        name="logf_cumsum",
    )(x, tri_incl)


def _top_blocks(scores, n_valid, n_blocks, topk):
    blk = lax.broadcasted_iota(jnp.int32, scores.shape, 0)
    rank = jnp.zeros(scores.shape, F32)
    for m in range(n_blocks):
        sm = scores[m:m + 1, :]
        beats = (sm > scores) | ((sm == scores) & (m < blk))
        rank = rank + jnp.where(beats, jnp.where(m < n_valid, 1.0, 0.0), 0.0)
    return jnp.where((rank < topk) & (blk < n_valid), 1.0, 0.0)


def _moba_kernel(q_ref, k_ref, v_ref, qf_ref, kf_ref, bias_ref, eye_ref, o_ref, km_ref, sel_ref, *, tq, nb):
    i = pl.program_id(2)

    @pl.when(i == 0)
    def _():
        lane = lax.broadcasted_iota(jnp.int32, (LANES, LANES), 1)
        sums = jnp.zeros((LANES, LANES), F32)
        for n in range(nb):
            col = jnp.sum(kf_ref[:, n * tq:(n + 1) * tq], axis=1, keepdims=True)
            sums = jnp.where(lane == n, col, sums)
        means = jnp.transpose(sums)[0:nb] * (1.0 / MOBA_BLOCK)
        lane = lane[0:nb]
        km_ref[0:nb, :] = jnp.where(lane < HEAD_DIM, means, 0.0)
        km_ref[nb:, :] = jnp.where(lane >= HEAD_DIM, means, 0.0)

    ka, kb2 = _split2(km_ref[...])
    qa, qb2 = _split2(qf_ref[...])
    scores = _dot_nt(ka, qa) + _dot_nt(kb2, qa) + _dot_nt(ka, qb2)
    sel = jnp.concatenate([_top_blocks(scores[h * nb:(h + 1) * nb], i, nb, MOBA_TOPK) for h in range(2)]
                          + [jnp.zeros((LANES - 2 * nb, tq), F32)], axis=0)
    sel_t = _dot_nt(eye_ref[...], sel.astype(BF16))
    for c in range(2 * nb):
        sel_ref[c] = jnp.where(sel_t[:, c:c + 1] > 0.5, 0.0, NEG)

    qh = _pair_queries(q_ref[...])
    row = lax.broadcasted_iota(jnp.int32, (tq, tq), 0)
    col = lax.broadcasted_iota(jnp.int32, (tq, tq), 1)
    causal = col <= row

    def tile(n, states, diag):
        kt, vt, _ = _kv_tile(k_ref, v_ref, n, tq)
        new = []
        for h in range(2):
            s = _dot(qh[h], kt)
            if diag:
                s = jnp.where(causal, s + bias_ref[h, 0], NEG)
            else:
                s = s + bias_ref[h, jnp.minimum(i - n, 2)] + sel_ref[h * nb + n]
            new.append(_softmax_tile(s, vt, states[h]))
        return tuple(new)

    states = tile(i, (_softmax_init(tq), _softmax_init(tq)), True)
    states = lax.fori_loop(0, i, lambda n, st: tile(n, st, False), states)
    _softmax_finish(states, o_ref)


def _moba_attention(qb, kbt, vbt, qf, kt, bias_tiles, eye, n_pairs, col0):
    b, t, _ = qb.shape
    tq = MOBA_BLOCK
    nb = t // tq
    q_spec, kv_spec, o_spec = _pair_specs(t, tq, col0)
    return pl.pallas_call(
        functools.partial(_moba_kernel, tq=tq, nb=nb),
        grid=(b, n_pairs, nb),
        in_specs=[q_spec, kv_spec, kv_spec, q_spec, kv_spec,
                  pl.BlockSpec((2, 3, tq, tq), lambda bi, p, i: (p, 0, 0, 0)),
                  _const_spec(eye.shape)],
        out_specs=o_spec,
        out_shape=jax.ShapeDtypeStruct((b, t, n_pairs * LANES), BF16),
        scratch_shapes=[pltpu.VMEM((2 * nb, LANES), F32), pltpu.VMEM((2 * nb, tq, 1), F32)],
        compiler_params=_params("parallel", "parallel", "arbitrary"),
        name="moba_attention",
    )(qb, kbt, vbt, qf, kt, bias_tiles, eye)


def _page_copies(pt_ref, b, slot, n_pages, pools_bufs_sems):
    cps = []
    for pool, buf, sem in pools_bufs_sems:
        for p in range(n_pages):
            cps.append(pltpu.make_async_copy(pool.at[pt_ref[b, p]], buf.at[slot, p], sem.at[slot, p]))
    return cps


def _gather_pipeline(pt_ref, n_pages, pools_bufs_sems):
    b = pl.program_id(0)
    slot = b % 2

    @pl.when(b == 0)
    def _():
        for cp in _page_copies(pt_ref, b, slot, n_pages, pools_bufs_sems):
            cp.start()

    @pl.when(b + 1 < pl.num_programs(0))
    def _():
        for cp in _page_copies(pt_ref, b + 1, 1 - slot, n_pages, pools_bufs_sems):
            cp.start()

    for cp in _page_copies(pt_ref, b, slot, n_pages, pools_bufs_sems):
        cp.wait()
    return slot


def _own_column(x, b):
    lane = lax.broadcasted_iota(jnp.int32, x.shape, x.ndim - 1)
    return jnp.sum(jnp.where(lane == b, x, 0.0), axis=x.ndim - 1, keepdims=True)


def _page_logits(kbuf, slot, q_col, n_pages, pages_per_chunk):
    q_bc = jnp.broadcast_to(q_col, q_col.shape[:2] + (PAGE_SIZE,))
    z = [jnp.sum(kbuf[slot, p] * q_bc, axis=1) for p in range(n_pages)]
    return [jnp.concatenate(z[c:c + pages_per_chunk], axis=1) for c in range(0, n_pages, pages_per_chunk)]


def _page_values(vbuf, slot, w_chunks, n_pages, pages_per_chunk, finish, o_ref):
    nh = w_chunks[0].shape[0]
    for g in range(0, nh, PV_HEAD_GROUP):
        rows = slice(g, g + PV_HEAD_GROUP)
        acc = jnp.zeros((PV_HEAD_GROUP, HEAD_DIM, PAGE_SIZE), F32)
        for p in range(n_pages):
            c, r = divmod(p, pages_per_chunk)
            w = w_chunks[c][rows, r * PAGE_SIZE:(r + 1) * PAGE_SIZE]
            acc = acc + vbuf[slot, p, rows] * w[:, None, :]
        o_ref[rows, :] = finish(rows, jnp.sum(acc, axis=2))


def _dec_even_kernel(pt_ref, qt_ref, knt_ref, vn_ref, dbias_ref, b0_ref, tri_ref, kc_ref, vc_ref,
                     o_ref, kbuf, vbuf, ksem, vsem, *, n_pages, n_sb, chunk):
    slot = _gather_pipeline(pt_ref, n_pages, [(kc_ref, kbuf, ksem), (vc_ref, vbuf, vsem)])
    b = pl.program_id(0)
    ppc = chunk // PAGE_SIZE
    nc = n_pages // ppc
    nh = qt_ref.shape[0]
    n_mb = nh - n_sb

    qt = qt_ref[...]
    zs = _page_logits(kbuf, slot, _own_column(qt, b), n_pages, ppc)
    z_new = _own_column(jnp.sum(qt * knt_ref[...], axis=1), b)

    tri = tri_ref[...]
    run = jnp.zeros((n_sb, 1), F32)
    w_sb = [None] * nc
    for c in reversed(range(nc)):
        lb, l1 = _log_sigmoid_parts(zs[c][0:n_sb])
        hi, lo = _split2(l1)
        tail = _dot(hi, tri) + _dot(lo, tri)
        w_sb[c] = jnp.exp(lb + tail + run)
        run = run + tail[:, 0:1] + l1[:, 0:1]

    sc = [jnp.sum(zs[n][n_sb:], axis=1, keepdims=True) * (1.0 / MOBA_BLOCK) for n in range(nc)]
    lg = []
    for n in range(nc):
        rank = jnp.zeros((n_mb, 1), F32)
        for m in range(nc):
            if m != n:
                beats = (sc[m] > sc[n]) | ((sc[m] == sc[n]) & (m < n))
                rank = rank + jnp.where(beats, 1.0, 0.0)
        mask = jnp.where(rank < MOBA_TOPK, 0.0, NEG)
        lg.append(zs[n][n_sb:] + dbias_ref[:, n * chunk:(n + 1) * chunk] + mask)
    lg_new = z_new[n_sb:] + b0_ref[...]
    m = lg_new
    for n in range(nc):
        m = jnp.maximum(m, jnp.max(lg[n], axis=1, keepdims=True))
    p_new = jnp.exp(lg_new - m)
    l = p_new
    w_all = []
    for c in range(nc):
        p = jnp.exp(lg[c] - m)
        l = l + jnp.sum(p, axis=1, keepdims=True)
        w_all.append(jnp.concatenate([w_sb[c], p], axis=0))
    p_new_all = jnp.concatenate([jnp.zeros((n_sb, 1), F32), p_new], axis=0)
    inv_l = jnp.concatenate([jnp.ones((n_sb, 1), F32), 1.0 / l], axis=0)
    finish = lambda rows, acc: (acc + p_new_all[rows] * vn_ref[rows, :]) * inv_l[rows]
    _page_values(vbuf, slot, w_all, n_pages, ppc, finish, o_ref)


def _dec_specs(nseq, nh):
    const = lambda shape: pl.BlockSpec(shape, lambda b, pt: (0,) * len(shape), pipeline_mode=pl.Buffered(1))
    per_seq = pl.BlockSpec((None, nh, HEAD_DIM), lambda b, pt: (b, 0, 0))
    return const, per_seq


def _dec_even(page_table, qt, knt, vn, dbias, b0, tri, k_pool, v_pool, n_sb):
    nseq, n_pages = page_table.shape
    nh = qt.shape[0]
    const, per_seq = _dec_specs(nseq, nh)
    page = (nh, HEAD_DIM, PAGE_SIZE)
    grid_spec = pltpu.PrefetchScalarGridSpec(
        num_scalar_prefetch=1,
        grid=(nseq,),
        in_specs=[const(qt.shape), const(knt.shape), per_seq, const(dbias.shape), const(b0.shape), const(tri.shape),
                  pl.BlockSpec(memory_space=pl.ANY), pl.BlockSpec(memory_space=pl.ANY)],
        out_specs=per_seq,
        scratch_shapes=[pltpu.VMEM((2, n_pages) + page, F32), pltpu.VMEM((2, n_pages) + page, F32),
                        pltpu.SemaphoreType.DMA((2, n_pages)), pltpu.SemaphoreType.DMA((2, n_pages))],
    )
    return pl.pallas_call(
        functools.partial(_dec_even_kernel, n_pages=n_pages, n_sb=n_sb, chunk=MOBA_BLOCK),
        grid_spec=grid_spec,
        out_shape=jax.ShapeDtypeStruct((nseq, nh, HEAD_DIM), F32),
        compiler_params=_params("arbitrary"),
        name="decode_even",
    )(page_table, qt, knt, vn, dbias, b0, tri, k_pool, v_pool)


def _dec_odd_kernel(pt_ref, qt_ref, knt_ref, vn_ref, lfn_ref, tri_ref, kc_ref, vc_ref, lc_ref,
                    o_ref, kbuf, vbuf, lbuf, ksem, vsem, lsem, *, n_pages, chunk):
    slot = _gather_pipeline(pt_ref, n_pages, [(kc_ref, kbuf, ksem), (vc_ref, vbuf, vsem), (lc_ref, lbuf, lsem)])
    b = pl.program_id(0)
    ppc = chunk // PAGE_SIZE
    nc = n_pages // ppc

    qt = qt_ref[...]
    zs = _page_logits(kbuf, slot, _own_column(qt, b), n_pages, ppc)
    lg_new = _own_column(jnp.sum(qt * knt_ref[...], axis=1), b)
    tri = tri_ref[...]
    run = _own_column(lfn_ref[...], b)
    lg = [None] * nc
    for c in reversed(range(nc)):
        lf_c = jnp.concatenate([lbuf[slot, p] for p in range(c * ppc, (c + 1) * ppc)], axis=1)
        a, b3, d = _split3(lf_c)
        tail = _dot(a, tri) + _dot(b3, tri) + _dot(d, tri)
        lg[c] = zs[c] + tail + run
        run = run + tail[:, 0:1] + lf_c[:, 0:1]
    m = lg_new
    for c in range(nc):
        m = jnp.maximum(m, jnp.max(lg[c], axis=1, keepdims=True))
    p_new = jnp.exp(lg_new - m)
    l = p_new
    w_all = []
    for c in range(nc):
        p = jnp.exp(lg[c] - m)
        l = l + jnp.sum(p, axis=1, keepdims=True)
        w_all.append(p)
    finish = lambda rows, acc: (acc + p_new[rows] * vn_ref[rows, :]) / l[rows]
    _page_values(vbuf, slot, w_all, n_pages, ppc, finish, o_ref)


def _dec_odd(page_table, qt, knt, vn, lfn, tri, k_pool, v_pool, lf_pool):
    nseq, n_pages = page_table.shape
    nh = qt.shape[0]
    const, per_seq = _dec_specs(nseq, nh)
    page = (nh, HEAD_DIM, PAGE_SIZE)
    grid_spec = pltpu.PrefetchScalarGridSpec(
        num_scalar_prefetch=1,
        grid=(nseq,),
        in_specs=[const(qt.shape), const(knt.shape), per_seq, const(lfn.shape), const(tri.shape),
                  pl.BlockSpec(memory_space=pl.ANY), pl.BlockSpec(memory_space=pl.ANY),
                  pl.BlockSpec(memory_space=pl.ANY)],
        out_specs=per_seq,
        scratch_shapes=[pltpu.VMEM((2, n_pages) + page, F32), pltpu.VMEM((2, n_pages) + page, F32),
                        pltpu.VMEM((2, n_pages, nh, PAGE_SIZE), F32),
                        pltpu.SemaphoreType.DMA((2, n_pages)), pltpu.SemaphoreType.DMA((2, n_pages)),
                        pltpu.SemaphoreType.DMA((2, n_pages))],
    )
    return pl.pallas_call(
        functools.partial(_dec_odd_kernel, n_pages=n_pages, chunk=MOBA_BLOCK),
        grid_spec=grid_spec,
        out_shape=jax.ShapeDtypeStruct((nseq, nh, HEAD_DIM), F32),
        compiler_params=_params("arbitrary"),
        name="decode_odd",
    )(page_table, qt, knt, vn, lfn, tri, k_pool, v_pool, lf_pool)


def _tri(n, keep):
    r = lax.broadcasted_iota(jnp.int32, (n, n), 0)
    c = lax.broadcasted_iota(jnp.int32, (n, n), 1)
    return keep(r, c).astype(BF16)


def _token_major(xt, nh):
    g, _, seq = xt.shape
    return jnp.transpose(xt.reshape(g, nh, HEAD_DIM, seq), (0, 3, 1, 2))


def kernel(x_prompt, x_sample, cache_k_even, cache_v_even, cache_k_odd, cache_v_odd, cache_logf_odd, page_table, attn_norm, ffn_norm, w_in_even, w_out_even, q_norm_even, k_norm_even, rel_bias, w_in_odd, b_f_odd, w_out_odd, q_norm_odd, k_norm_odd, w_gate, w_up, w_down):
    bsz, seq, d = x_prompt.shape
    nseq, dec_seq, _ = x_sample.shape
    depth = attn_norm.shape[0]
    page = cache_k_even.shape[2]
    nh = cache_k_even.shape[3]
    dm = nh * HEAD_DIM
    n_mb = rel_bias.shape[1]
    n_sb = nh - n_mb
    half = n_sb * HEAD_DIM
    n_pages = page_table.shape[1]
    past = n_pages * page
    assert dec_seq == 1 and page == PAGE_SIZE and n_sb == n_mb and n_sb % PV_HEAD_GROUP == 0
    assert seq % MOBA_BLOCK == 0 and past % MOBA_BLOCK == 0 and half % MXU_DIM == 0

    tri_after = _tri(MOBA_BLOCK, lambda r, c: r > c)
    tri_incl = _tri(MOBA_BLOCK, lambda r, c: r <= c)
    eye_blk = _tri(MOBA_BLOCK, lambda r, c: r == c)
    seg = _tri(MXU_DIM, lambda r, c: (r // HEAD_DIM) == (c // HEAD_DIM)) * (1.0 / HEAD_DIM)
    if SB_TILE != MOBA_BLOCK:
        tri_sb = _tri(SB_TILE, lambda r, c: r > c)
    else:
        tri_sb = tri_after

    pool_view = lambda cache: jnp.transpose(cache, (0, 2, 3, 1))
    heads_first = lambda a: jnp.transpose(a.reshape(nseq, nh, HEAD_DIM), (1, 2, 0))

    xp = x_prompt.reshape(bsz * seq, d)
    xs = x_sample.reshape(nseq, d)
    bias_tiles, dec_bias = _bias_tables(rel_bias, past)
    dec_bias = dec_bias.reshape(n_mb, past)
    b0 = rel_bias[0].reshape(n_mb, 1)

    outs = {name: [] for name in ("ek_p", "ev_p", "ek_s", "ev_s", "ok_p", "ov_p", "of_p", "ok_s", "ov_s", "of_s")}
    r3 = lambda a: a.reshape(bsz, seq, a.shape[-1])
    sample_rows = lambda xt: jnp.transpose(xt[0]).reshape(nseq, nh, HEAD_DIM)
    for l in range(depth):
        i = l // 2
        g_attn = attn_norm[l].reshape(1, d)
        if l % 2 == 0:
            w = w_in_even[i]
            cols = lambda j: w[:, j * half:(j + 1) * half]
            wq = jnp.concatenate([cols(0), cols(3)], axis=1).astype(BF16)
            wkv = jnp.transpose(jnp.concatenate([cols(1), cols(4), cols(2), cols(5)], axis=1)).astype(BF16)
            gq = jnp.tile(q_norm_even[i], n_mb).reshape(1, half)
            gk = jnp.tile(k_norm_even[i], n_mb).reshape(half, 1)
            kt, vt, kbt, vbt, qb, qf = _proj_even(xp, g_attn, wq, wkv, gq, gk, seg, bsz)
            o_sb = _sb_attention(r3(qb), kbt, vbt, tri_sb, n_sb // 2, SB_TILE)
            o_mb = _moba_attention(r3(qb), kbt, vbt, r3(qf), kt, bias_tiles, eye_blk, n_mb // 2, half // LANES)
            outs["ek_p"].append(_token_major(kt, nh))
            outs["ev_p"].append(_token_major(vt, nh))
            kt_s, vt_s, _, _, _, qf_s = _proj_even(xs, g_attn, wq, wkv, gq, gk, seg, 1)
            o_s = _dec_even(page_table, heads_first(qf_s), kt_s.reshape(nh, HEAD_DIM, nseq), sample_rows(vt_s),
                            dec_bias, b0, tri_after, pool_view(cache_k_even[i]), pool_view(cache_v_even[i]),
                            n_sb).reshape(nseq, dm)
            outs["ek_s"].append(_token_major(kt_s, nh).reshape(nseq, 1, nh, HEAD_DIM))
            outs["ev_s"].append(_token_major(vt_s, nh).reshape(nseq, 1, nh, HEAD_DIM))
            w_out = w_out_even[i].astype(BF16)
            o_p = (o_sb.reshape(bsz * seq, half), o_mb.reshape(bsz * seq, half), 0, 0)
        else:
            w = w_in_odd[i]
            wq = w[:, 0:dm].astype(BF16)
            wkv = jnp.pad(jnp.transpose(w[:, dm:]), ((0, LANES - nh), (0, 0))).astype(BF16)
            b_f = jnp.pad(b_f_odd[i], (0, LANES - nh)).reshape(LANES, 1)
            gq = jnp.tile(q_norm_odd[i], nh).reshape(1, dm)
            gk = jnp.tile(k_norm_odd[i], nh).reshape(dm, 1)
            kt, vt, kbt, vbt, qb, _, lft = _proj_odd(xp, g_attn, wq, wkv, gq, gk, b_f, seg, nh, bsz)
            c_t = _cumsum_lanes(lft, tri_incl)
            o = _fox_attention(r3(qb), kbt, vbt, c_t.reshape(bsz, nh // 2, 2, seq), FOX_TILE)
            outs["ok_p"].append(_token_major(kt, nh))
            outs["ov_p"].append(_token_major(vt, nh))
            outs["of_p"].append(jnp.transpose(lft, (0, 2, 1)))
            kt_s, vt_s, _, _, _, qf_s, lft_s = _proj_odd(xs, g_attn, wq, wkv, gq, gk, b_f, seg, nh, 1)
            o_s = _dec_odd(page_table, heads_first(qf_s), kt_s.reshape(nh, HEAD_DIM, nseq), sample_rows(vt_s),
                           lft_s[0], tri_after, pool_view(cache_k_odd[i]), pool_view(cache_v_odd[i]),
                           jnp.transpose(cache_logf_odd[i], (0, 2, 1))).reshape(nseq, dm)
            outs["ok_s"].append(_token_major(kt_s, nh).reshape(nseq, 1, nh, HEAD_DIM))
            outs["ov_s"].append(_token_major(vt_s, nh).reshape(nseq, 1, nh, HEAD_DIM))
            outs["of_s"].append(jnp.transpose(lft_s[0]).reshape(nseq, 1, nh))
            w_out = w_out_odd[i].astype(BF16)
            o2 = o.reshape(bsz * seq, dm)
            o_p = (o2, o2, 0, 1)
        g_ffn = ffn_norm[l].reshape(1, d)
        wg, wu, wd = w_gate[l].astype(BF16), w_up[l].astype(BF16), w_down[l].astype(BF16)
        xp = _post(xp, *o_p, w_out, g_ffn, wg, wu, wd)
        xs = _post(xs, o_s, o_s, 0, 1, w_out, g_ffn, wg, wu, wd)

    st = lambda name: jnp.stack(outs[name])
    return (xp.reshape(bsz, seq, d), xs.reshape(nseq, 1, d),
            st("ek_p"), st("ev_p"), st("ek_s"), st("ev_s"),
            st("ok_p"), st("ov_p"), st("of_p"), st("ok_s"), st("ov_s"), st("of_s"))
```

```python
import functools
import math

import jax
import jax.numpy as jnp
from jax import lax
from jax.experimental import pallas as pl
from jax.experimental.pallas import tpu as pltpu

F32 = jnp.float32
BF16 = jnp.bfloat16

HEAD_DIM = 64
EPS = 1e-6
SCALE = HEAD_DIM ** -0.5
MOBA_BLOCK = 256
MOBA_TOPK = 3
PAGE_SIZE = 128
REL_BUCKETS = 32
REL_MAX_DIST = 128
NEG = -1e30
LANES = 128
MXU_DIM = 256
VMEM_LIMIT = 56 * 1024 * 1024
ROW_TILE = 256
SB_TILE = 512
MOBA_TILE = 512
FOX_TILE = 512
PV_HEAD_GROUP = 4

NT = (((1,), (1,)), ((), ()))


def _bucket_thresholds():
    max_exact = REL_BUCKETS // 2
    def bucket(d):
        return min(max_exact + int(math.log(d / max_exact) / math.log(REL_MAX_DIST / max_exact)
                                   * (REL_BUCKETS - max_exact)), REL_BUCKETS - 1)
    return [min(d for d in range(max_exact, REL_MAX_DIST + 1) if bucket(d) >= b)
            for b in range(max_exact + 1, REL_BUCKETS)]


_BUCKET_THRESHOLDS = _bucket_thresholds()


def _params(*sem):
    return pltpu.CompilerParams(dimension_semantics=sem, vmem_limit_bytes=VMEM_LIMIT)


def _const_spec(shape):
    return pl.BlockSpec(shape, lambda *_: (0,) * len(shape), pipeline_mode=pl.Buffered(1))


def _log_sigmoid_parts(z, small_relative=True):
    e = jnp.exp(-jnp.abs(z))
    sp = jnp.log1p(e) if small_relative else jnp.log(1.0 + e)
    return jnp.minimum(z, 0.0) - sp, jnp.minimum(-z, 0.0) - sp


def _split2(x):
    hi = x.astype(BF16)
    lo = (x - hi.astype(F32)).astype(BF16)
    return hi, lo


def _split3(x):
    a = x.astype(BF16)
    r = x - a.astype(F32)
    b = r.astype(BF16)
    c = (r - b.astype(F32)).astype(BF16)
    return a, b, c


def _dot(a, b):
    return jnp.dot(a, b, preferred_element_type=F32)


def _dot_nt(a, b):
    return lax.dot_general(a, b, NT, preferred_element_type=F32)


def _rms(x, g):
    ms = jnp.mean(x * x, axis=-1, keepdims=True)
    return x * lax.rsqrt(ms + EPS) * g


def _head_rms(t, gain, seg):
    outs = []
    for c in range(t.shape[1] // MXU_DIM):
        tc = t[:, c * MXU_DIM:(c + 1) * MXU_DIM]
        hi, lo = _split2(tc * tc)
        ms = _dot(hi, seg) + _dot(lo, seg)
        outs.append(tc * lax.rsqrt(ms + EPS) * gain[:, c * MXU_DIM:(c + 1) * MXU_DIM])
    return jnp.concatenate(outs, axis=1)


def _head_rms_t(t, gain, seg):
    outs = []
    for c in range(t.shape[0] // MXU_DIM):
        tc = t[c * MXU_DIM:(c + 1) * MXU_DIM]
        hi, lo = _split2(tc * tc)
        ms = _dot(seg, hi) + _dot(seg, lo)
        outs.append(tc * lax.rsqrt(ms + EPS) * gain[c * MXU_DIM:(c + 1) * MXU_DIM])
    return jnp.concatenate(outs, axis=0)


def _proj_even_kernel(x_ref, g_ref, wq_ref, wkv_ref, gq_ref, gk_ref, seg_ref,
                      kt_ref, vt_ref, kbt_ref, vbt_ref, qb_ref, qf_ref, *, half):
    h = _rms(x_ref[...], g_ref[...]).astype(BF16)
    seg = seg_ref[...]
    uq = _dot(h, wq_ref[...])
    q = jnp.concatenate([uq[:, 0:half], _head_rms(uq[:, half:], gq_ref[...], seg)], axis=1) * SCALE
    qf_ref[...] = q
    qb_ref[...] = q.astype(BF16)
    ukv = _dot_nt(wkv_ref[...], h)
    k = jnp.concatenate([ukv[0:half], _head_rms_t(ukv[half:2 * half], gk_ref[...], seg)], axis=0)
    v = ukv[2 * half:]
    kt_ref[...] = k
    vt_ref[...] = v
    kbt_ref[...] = k.astype(BF16)
    vbt_ref[...] = v.astype(BF16)


def _feature_major_specs(groups, seq, tm, feats):
    nt = seq // tm
    spec = pl.BlockSpec((None, feats, tm), lambda i: (i // nt, 0, i % nt))
    return spec, lambda dt: jax.ShapeDtypeStruct((groups, feats, seq), dt)


def _proj_even(x, g, wq, wkv, gq, gk_col, seg, groups):
    rows, d = x.shape
    dm = wq.shape[1]
    half = dm // 2
    seq = rows // groups
    tm = min(ROW_TILE, seq)
    row = lambda w: pl.BlockSpec((tm, w), lambda i: (i, 0))
    fm, fm_shape = _feature_major_specs(groups, seq, tm, dm)
    return pl.pallas_call(
        functools.partial(_proj_even_kernel, half=half),
        grid=(rows // tm,),
        in_specs=[row(d), _const_spec((1, d)), _const_spec(wq.shape), _const_spec(wkv.shape),
                  _const_spec((1, half)), _const_spec((half, 1)), _const_spec(seg.shape)],
        out_specs=[fm, fm, fm, fm, row(dm), row(dm)],
        out_shape=[fm_shape(F32), fm_shape(F32), fm_shape(BF16), fm_shape(BF16),
                   jax.ShapeDtypeStruct((rows, dm), BF16), jax.ShapeDtypeStruct((rows, dm), F32)],
        compiler_params=_params("parallel"),
        name="proj_even",
    )(x, g, wq, wkv, gq, gk_col, seg)


def _proj_odd_kernel(x_ref, g_ref, wq_ref, wkv_ref, gq_ref, gk_ref, bf_ref, seg_ref,
                     kt_ref, vt_ref, kbt_ref, vbt_ref, qb_ref, qf_ref, lft_ref, *, dm, nh):
    h = _rms(x_ref[...], g_ref[...]).astype(BF16)
    seg = seg_ref[...]
    q = _head_rms(_dot(h, wq_ref[...]), gq_ref[...], seg) * SCALE
    qf_ref[...] = q
    qb_ref[...] = q.astype(BF16)
    ukv = _dot_nt(wkv_ref[...], h)
    k = _head_rms_t(ukv[0:dm], gk_ref[...], seg)
    v = ukv[dm:2 * dm]
    kt_ref[...] = k
    vt_ref[...] = v
    kbt_ref[...] = k.astype(BF16)
    vbt_ref[...] = v.astype(BF16)
    lf, _ = _log_sigmoid_parts(ukv[2 * dm:] + bf_ref[...])
    lft_ref[...] = lf[0:nh]


def _proj_odd(x, g, wq, wkv, gq, gk_col, bf_col, seg, nh, groups):
    rows, d = x.shape
    dm = wq.shape[1]
    seq = rows // groups
    tm = min(ROW_TILE, seq)
    row = lambda w: pl.BlockSpec((tm, w), lambda i: (i, 0))
    fm, fm_shape = _feature_major_specs(groups, seq, tm, dm)
    lfm, lfm_shape = _feature_major_specs(groups, seq, tm, nh)
    return pl.pallas_call(
        functools.partial(_proj_odd_kernel, dm=dm, nh=nh),
        grid=(rows // tm,),
        in_specs=[row(d), _const_spec((1, d)), _const_spec(wq.shape), _const_spec(wkv.shape),
                  _const_spec((1, dm)), _const_spec((dm, 1)), _const_spec((LANES, 1)), _const_spec(seg.shape)],
        out_specs=[fm, fm, fm, fm, row(dm), row(dm), lfm],
        out_shape=[fm_shape(F32), fm_shape(F32), fm_shape(BF16), fm_shape(BF16),
                   jax.ShapeDtypeStruct((rows, dm), BF16), jax.ShapeDtypeStruct((rows, dm), F32), lfm_shape(F32)],
        compiler_params=_params("parallel"),
        name="proj_odd",
    )(x, g, wq, wkv, gq, gk_col, bf_col, seg)


def _post_kernel(x_ref, oa_ref, ob_ref, wo_ref, g_ref, wg_ref, wu_ref, wd_ref, y_ref):
    ha = oa_ref.shape[1]
    x1 = (x_ref[...] + _dot(oa_ref[...].astype(BF16), wo_ref[0:ha, :])
          + _dot(ob_ref[...].astype(BF16), wo_ref[ha:, :]))
    h = _rms(x1, g_ref[...]).astype(BF16)
    gate = _dot(h, wg_ref[...])
    up = _dot(h, wu_ref[...])
    act = (gate * (1.0 / (1.0 + jnp.exp(-gate))) * up).astype(BF16)
    y_ref[...] = x1 + _dot(act, wd_ref[...])


def _post(x, oa, ob, oa_col, ob_col, wo, g, wg, wu, wd):
    rows, d = x.shape
    half = wo.shape[0] // 2
    tm = min(ROW_TILE, rows)
    return pl.pallas_call(
        _post_kernel,
        grid=(rows // tm,),
        in_specs=[pl.BlockSpec((tm, d), lambda i: (i, 0)),
                  pl.BlockSpec((tm, half), lambda i: (i, oa_col)),
                  pl.BlockSpec((tm, half), lambda i: (i, ob_col)),
                  _const_spec(wo.shape), _const_spec((1, d)), _const_spec(wg.shape),
                  _const_spec(wu.shape), _const_spec(wd.shape)],
        out_specs=pl.BlockSpec((tm, d), lambda i: (i, 0)),
        out_shape=jax.ShapeDtypeStruct((rows, d), F32),
        compiler_params=_params("parallel"),
        name="post_attn",
    )(x, oa, ob, wo, g, wg, wu, wd)


def _rel_bias_of(dist, rel_ref, h):
    d = jnp.maximum(dist, 0)
    big = jnp.full(d.shape, REL_BUCKETS // 2, jnp.int32)
    for thr in _BUCKET_THRESHOLDS:
        big = big + jnp.where(d >= thr, 1, 0)
    bucket = jnp.where(d < REL_BUCKETS // 2, d, big)
    out = jnp.full(d.shape, rel_ref[REL_BUCKETS - 1, h], F32)
    for b in range(REL_BUCKETS - 1):
        out = jnp.where(bucket == b, rel_ref[b, h], out)
    return out


def _bias_kernel(rel_ref, tiles_ref, dec_ref, *, blk, past_len):
    h = pl.program_id(0)
    row = lax.broadcasted_iota(jnp.int32, (blk, blk), 0)
    col = lax.broadcasted_iota(jnp.int32, (blk, blk), 1)
    tiles_ref[0] = _rel_bias_of(row - col, rel_ref, h)
    tiles_ref[1] = _rel_bias_of(row - col + blk, rel_ref, h)
    tiles_ref[2] = jnp.full((blk, blk), rel_ref[REL_BUCKETS - 1, h], F32)
    pos = lax.broadcasted_iota(jnp.int32, (1, past_len), 1)
    dec_ref[...] = _rel_bias_of(past_len - pos, rel_ref, h)


def _bias_tables(rel_bias, past_len, blk):
    assert blk + 1 >= REL_MAX_DIST, "tile 2 must be all in the last bucket"
    nh = rel_bias.shape[1]
    return pl.pallas_call(
        functools.partial(_bias_kernel, blk=blk, past_len=past_len),
        grid=(nh,),
        in_specs=[pl.BlockSpec(memory_space=pltpu.SMEM)],
        out_specs=[pl.BlockSpec((None, 3, blk, blk), lambda h: (h, 0, 0, 0)),
                   pl.BlockSpec((None, 1, past_len), lambda h: (h, 0, 0))],
        out_shape=[jax.ShapeDtypeStruct((nh, 3, blk, blk), F32),
                   jax.ShapeDtypeStruct((nh, 1, past_len), F32)],
        compiler_params=_params("parallel"),
        name="rel_bias_tables",
    )(rel_bias)


def _pair_queries(q2):
    lane = lax.broadcasted_iota(jnp.int32, q2.shape, 1)
    qf = q2.astype(F32)
    return [jnp.where(lane < HEAD_DIM, qf, 0.0).astype(q2.dtype), jnp.where(lane >= HEAD_DIM, qf, 0.0).astype(q2.dtype)]


def _pair_merge(a0, a1):
    lane = lax.broadcasted_iota(jnp.int32, a0.shape, 1)
    return jnp.where(lane < HEAD_DIM, a0, a1)


def _kv_tile(k_ref, v_ref, j, tk):
    start = pl.multiple_of(j * tk, tk)
    return k_ref[:, pl.ds(start, tk)], v_ref[:, pl.ds(start, tk)], start


def _sb_kernel(q_ref, k_ref, v_ref, tri_ref, o_ref, *, tq, tk):
    i = pl.program_id(2)
    qh = _pair_queries(q_ref[...])
    tri = tri_ref[...]
    row = lax.broadcasted_iota(jnp.int32, (tq, tk), 0)
    col = lax.broadcasted_iota(jnp.int32, (tq, tk), 1)

    def block(start, carry, mask):
        kt = k_ref[:, pl.ds(start, tk)]
        vt = v_ref[:, pl.ds(start, tk)]
        new = []
        for h in range(2):
            run, acc = carry[h]
            lb, l1 = _log_sigmoid_parts(_dot(qh[h], kt), small_relative=False)
            if mask is not None:
                l1 = jnp.where(mask, l1, 0.0)
            tail = _dot(l1.astype(BF16), tri)
            w = jnp.exp(lb + tail + run)
            if mask is not None:
                w = jnp.where(mask, w, 0.0)
            acc = acc + _dot_nt(w.astype(BF16), vt)
            run = run + tail[:, 0:1] + l1[:, 0:1]
            new.append((run, acc))
        return tuple(new)

    def tile(j, carry, diag):
        for blk in reversed(range(tq // tk)):
            mask = (col + blk * tk < row) if diag else None
            carry = block(pl.multiple_of(j * tq + blk * tk, tk), carry, mask)
        return carry

    init = tuple((jnp.zeros((tq, 1), F32), jnp.zeros((tq, LANES), F32)) for _ in range(2))
    carry = tile(i, init, True)
    carry = lax.fori_loop(0, i, lambda t, c: tile(i - 1 - t, c, False), carry)
    o_ref[...] = _pair_merge(carry[0][1], carry[1][1]).astype(o_ref.dtype)


def _pair_specs(t, tq, col0):
    q = pl.BlockSpec((None, tq, LANES), lambda bi, p, i: (bi, i, col0 + p))
    kv = pl.BlockSpec((None, LANES, t), lambda bi, p, i: (bi, col0 + p, 0))
    o = pl.BlockSpec((None, tq, LANES), lambda bi, p, i: (bi, i, p))
    return q, kv, o


def _sb_attention(qb, kbt, vbt, tri, n_pairs, tq):
    b, t, _ = qb.shape
    q_spec, kv_spec, o_spec = _pair_specs(t, tq, 0)
    return pl.pallas_call(
        functools.partial(_sb_kernel, tq=tq, tk=tri.shape[0]),
        grid=(b, n_pairs, t // tq),
        in_specs=[q_spec, kv_spec, kv_spec, _const_spec(tri.shape)],
        out_specs=o_spec,
        out_shape=jax.ShapeDtypeStruct((b, t, n_pairs * LANES), BF16),
        compiler_params=_params("parallel", "parallel", "arbitrary"),
        name="sb_attention",
    )(qb, kbt, vbt, tri)


def _softmax_tile(s, vt, state):
    m, l, acc = state
    m_new = jnp.maximum(m, jnp.max(s, axis=1, keepdims=True))
    alpha = jnp.exp(m - m_new)
    p = jnp.exp(s - m_new)
    l = alpha * l + jnp.sum(p, axis=1, keepdims=True)
    acc = alpha * acc + _dot_nt(p.astype(BF16), vt)
    return m_new, l, acc


def _softmax_init(tq):
    return (jnp.full((tq, 1), NEG, F32), jnp.zeros((tq, 1), F32), jnp.zeros((tq, LANES), F32))


def _softmax_finish(states, o_ref):
    outs = [acc / l for (_, l, acc) in states]
    o_ref[...] = _pair_merge(outs[0], outs[1]).astype(o_ref.dtype)


def _fox_kernel(q_ref, k_ref, v_ref, ck_ref, o_ref, *, tq):
    i = pl.program_id(2)
    qh = _pair_queries(q_ref[...])
    row = lax.broadcasted_iota(jnp.int32, (tq, tq), 0)
    col = lax.broadcasted_iota(jnp.int32, (tq, tq), 1)
    causal = col <= row

    def tile(j, states, diag):
        kt, vt, start = _kv_tile(k_ref, v_ref, j, tq)
        new = []
        for h in range(2):
            s = _dot(qh[h], kt) - ck_ref[h:h + 1, pl.ds(start, tq)]
            if diag:
                s = jnp.where(causal, s, NEG)
            new.append(_softmax_tile(s, vt, states[h]))
        return tuple(new)

    states = tile(i, (_softmax_init(tq), _softmax_init(tq)), True)
    states = lax.fori_loop(0, i, lambda j, st: tile(j, st, False), states)
    _softmax_finish(states, o_ref)


def _fox_attention(qb, kbt, vbt, ck, tq):
    b, t, dm = qb.shape
    n_pairs = dm // LANES
    q_spec, kv_spec, o_spec = _pair_specs(t, tq, 0)
    return pl.pallas_call(
        functools.partial(_fox_kernel, tq=tq),
        grid=(b, n_pairs, t // tq),
        in_specs=[q_spec, kv_spec, kv_spec,
                  pl.BlockSpec((None, None, 2, t), lambda bi, p, i: (bi, p, 0, 0))],
        out_specs=o_spec,
        out_shape=jax.ShapeDtypeStruct((b, t, dm), BF16),
        compiler_params=_params("parallel", "parallel", "arbitrary"),
        name="fox_attention",
    )(qb, kbt, vbt, ck)


def _cumsum_kernel(x_ref, tri_ref, o_ref, *, chunk):
    tri = tri_ref[...]
    carry = jnp.zeros((x_ref.shape[0], 1), F32)
    for c in range(x_ref.shape[1] // chunk):
        a, b, d = _split3(x_ref[:, c * chunk:(c + 1) * chunk])
        cs = _dot(a, tri) + _dot(b, tri) + _dot(d, tri) + carry
        o_ref[:, c * chunk:(c + 1) * chunk] = cs
        carry = cs[:, chunk - 1:chunk]


def _cumsum_lanes(x, tri_incl):
    b, nh, t = x.shape
    return pl.pallas_call(
        functools.partial(_cumsum_kernel, chunk=tri_incl.shape[0]),
        grid=(b,),
        in_specs=[pl.BlockSpec((None, nh, t), lambda bi: (bi, 0, 0)), _const_spec(tri_incl.shape)],
        out_specs=pl.BlockSpec((None, nh, t), lambda bi: (bi, 0, 0)),
        out_shape=jax.ShapeDtypeStruct((b, nh, t), F32),
        compiler_params=_params("parallel"),
        name="logf_cumsum",
    )(x, tri_incl)


def _top_blocks(scores, n_valid, n_blocks, topk):
    blk = lax.broadcasted_iota(jnp.int32, scores.shape, 0)
    rank = jnp.zeros(scores.shape, F32)
    for m in range(n_blocks):
        sm = scores[m:m + 1, :]
        beats = (sm > scores) | ((sm == scores) & (m < blk))
        rank = rank + jnp.where(beats, jnp.where(m < n_valid, 1.0, 0.0), 0.0)
    return jnp.where(((rank < topk) & (blk < n_valid)) | (blk == n_valid), 1.0, 0.0)


def _moba_kernel(q_ref, k_ref, v_ref, qf_ref, kf_ref, bias_ref, o_ref, km_ref, kaug_ref, *, tq, nb):
    i = pl.program_id(2)
    t = k_ref.shape[1]
    blk_shift = MOBA_BLOCK.bit_length() - 1

    @pl.when(i == 0)
    def _():
        lane = lax.broadcasted_iota(jnp.int32, (LANES, LANES), 1)
        sums = jnp.zeros((LANES, LANES), F32)
        for n in range(nb):
            col = jnp.sum(kf_ref[:, n * MOBA_BLOCK:(n + 1) * MOBA_BLOCK], axis=1, keepdims=True)
            sums = jnp.where(lane == n, col, sums)
        means = jnp.transpose(sums)[0:nb] * (1.0 / MOBA_BLOCK)
        lane = lane[0:nb]
        km_ref[0:nb, :] = jnp.where(lane < HEAD_DIM, means, 0.0)
        km_ref[nb:, :] = jnp.where(lane >= HEAD_DIM, means, 0.0)
        kaug_ref[0:LANES, :] = k_ref[...]
        c = lax.broadcasted_iota(jnp.int32, (LANES, t), 0)
        s = lax.broadcasted_iota(jnp.int32, (LANES, t), 1)
        ind = (c < 2 * nb) & (jnp.where(c >= nb, c - nb, c) == jnp.right_shift(s, blk_shift))
        kaug_ref[LANES:, :] = jnp.where(ind, 1.0, 0.0).astype(BF16)

    ka, kb2 = _split2(km_ref[...])
    qa, qb2 = _split2(qf_ref[...])
    scores = _dot_nt(ka, qa) + _dot_nt(kb2, qa) + _dot_nt(ka, qb2)
    pos = lax.broadcasted_iota(jnp.int32, (1, tq), 1)
    own = i * (tq // MOBA_BLOCK) + jnp.right_shift(pos, blk_shift)
    sel = jnp.concatenate([_top_blocks(scores[h * nb:(h + 1) * nb], own, nb, MOBA_TOPK) for h in range(2)]
                          + [jnp.zeros((LANES - 2 * nb, tq), F32)], axis=0)
    sel_t = jnp.transpose(sel)
    colq = lax.broadcasted_iota(jnp.int32, (tq, LANES), 1)
    qh = _pair_queries(q_ref[...])
    qaug = []
    for h in range(2):
        mine = (colq >= h * nb) & (colq < (h + 1) * nb)
        neg = jnp.where(mine, jnp.where(sel_t < 0.5, NEG, 0.0), 0.0)
        qaug.append(jnp.concatenate([qh[h], neg.astype(BF16)], axis=1))

    row = lax.broadcasted_iota(jnp.int32, (tq, tq), 0)
    col = lax.broadcasted_iota(jnp.int32, (tq, tq), 1)
    causal = col <= row

    def tile(j, states, diag):
        start = pl.multiple_of(j * tq, tq)
        kt = kaug_ref[:, pl.ds(start, tq)]
        vt = v_ref[:, pl.ds(start, tq)]
        new = []
        for h in range(2):
            s = _dot(qaug[h], kt) + bias_ref[h, jnp.minimum(i - j, 2)]
            if diag:
                s = jnp.where(causal, s, NEG)
            new.append(_softmax_tile(s, vt, states[h]))
        return tuple(new)

    states = tile(i, (_softmax_init(tq), _softmax_init(tq)), True)
    states = lax.fori_loop(0, i, lambda j, st: tile(j, st, False), states)
    _softmax_finish(states, o_ref)


def _moba_attention(qb, kbt, vbt, qf, kt, bias_tiles, n_pairs, col0):
    b, t, _ = qb.shape
    tq = bias_tiles.shape[-1]
    nb = t // MOBA_BLOCK
    q_spec, kv_spec, o_spec = _pair_specs(t, tq, col0)
    return pl.pallas_call(
        functools.partial(_moba_kernel, tq=tq, nb=nb),
        grid=(b, n_pairs, t // tq),
        in_specs=[q_spec, kv_spec, kv_spec, q_spec, kv_spec,
                  pl.BlockSpec((2, 3, tq, tq), lambda bi, p, i: (p, 0, 0, 0))],
        out_specs=o_spec,
        out_shape=jax.ShapeDtypeStruct((b, t, n_pairs * LANES), BF16),
        scratch_shapes=[pltpu.VMEM((2 * nb, LANES), F32), pltpu.VMEM((2 * LANES, t), BF16)],
        compiler_params=_params("parallel", "parallel", "arbitrary"),
        name="moba_attention",
    )(qb, kbt, vbt, qf, kt, bias_tiles)


def _page_copies(pt_ref, b, slot, n_pages, pools_bufs_sems):
    cps = []
    for pool, buf, sem in pools_bufs_sems:
        for p in range(n_pages):
            cps.append(pltpu.make_async_copy(pool.at[pt_ref[b, p]], buf.at[slot, p], sem.at[slot, p]))
    return cps


def _gather_pipeline(pt_ref, n_pages, pools_bufs_sems):
    b = pl.program_id(0)
    slot = b % 2

    @pl.when(b == 0)
    def _():
        for cp in _page_copies(pt_ref, b, slot, n_pages, pools_bufs_sems):
            cp.start()

    @pl.when(b + 1 < pl.num_programs(0))
    def _():
        for cp in _page_copies(pt_ref, b + 1, 1 - slot, n_pages, pools_bufs_sems):
            cp.start()

    for cp in _page_copies(pt_ref, b, slot, n_pages, pools_bufs_sems):
        cp.wait()
    return slot


def _own_column(x, b):
    lane = lax.broadcasted_iota(jnp.int32, x.shape, x.ndim - 1)
    return jnp.sum(jnp.where(lane == b, x, 0.0), axis=x.ndim - 1, keepdims=True)


def _page_logits(kbuf, slot, q_col, n_pages, pages_per_chunk):
    q_bc = jnp.broadcast_to(q_col, q_col.shape[:2] + (PAGE_SIZE,))
    z = [jnp.sum(kbuf[slot, p] * q_bc, axis=1) for p in range(n_pages)]
    return [jnp.concatenate(z[c:c + pages_per_chunk], axis=1) for c in range(0, n_pages, pages_per_chunk)]


def _page_values(vbuf, slot, w_chunks, n_pages, pages_per_chunk, finish, o_ref):
    nh = w_chunks[0].shape[0]
    for g in range(0, nh, PV_HEAD_GROUP):
        rows = slice(g, g + PV_HEAD_GROUP)
        acc = jnp.zeros((PV_HEAD_GROUP, HEAD_DIM, PAGE_SIZE), F32)
        for p in range(n_pages):
            c, r = divmod(p, pages_per_chunk)
            w = w_chunks[c][rows, r * PAGE_SIZE:(r + 1) * PAGE_SIZE]
            acc = acc + vbuf[slot, p, rows] * w[:, None, :]
        o_ref[rows, :] = finish(rows, jnp.sum(acc, axis=2))


def _dec_even_kernel(pt_ref, qt_ref, knt_ref, vn_ref, dbias_ref, b0_ref, tri_ref, kc_ref, vc_ref,
                     o_ref, kbuf, vbuf, ksem, vsem, *, n_pages, n_sb, chunk):
    slot = _gather_pipeline(pt_ref, n_pages, [(kc_ref, kbuf, ksem), (vc_ref, vbuf, vsem)])
    b = pl.program_id(0)
    ppc = chunk // PAGE_SIZE
    nc = n_pages // ppc
    nh = qt_ref.shape[0]
    n_mb = nh - n_sb

    qt = qt_ref[...]
    zs = _page_logits(kbuf, slot, _own_column(qt, b), n_pages, ppc)
    z_new = _own_column(jnp.sum(qt * knt_ref[...], axis=1), b)

    tri = tri_ref[...]
    run = jnp.zeros((n_sb, 1), F32)
    w_sb = [None] * nc
    for c in reversed(range(nc)):
        lb, l1 = _log_sigmoid_parts(zs[c][0:n_sb])
        hi, lo = _split2(l1)
        tail = _dot(hi, tri) + _dot(lo, tri)
        w_sb[c] = jnp.exp(lb + tail + run)
        run = run + tail[:, 0:1] + l1[:, 0:1]

    sc = [jnp.sum(zs[n][n_sb:], axis=1, keepdims=True) * (1.0 / MOBA_BLOCK) for n in range(nc)]
    lg = []
    for n in range(nc):
        rank = jnp.zeros((n_mb, 1), F32)
        for m in range(nc):
            if m != n:
                beats = (sc[m] > sc[n]) | ((sc[m] == sc[n]) & (m < n))
                rank = rank + jnp.where(beats, 1.0, 0.0)
        mask = jnp.where(rank < MOBA_TOPK, 0.0, NEG)
        lg.append(zs[n][n_sb:] + dbias_ref[:, n * chunk:(n + 1) * chunk] + mask)
    lg_new = z_new[n_sb:] + b0_ref[...]
    m = lg_new
    for n in range(nc):
        m = jnp.maximum(m, jnp.max(lg[n], axis=1, keepdims=True))
    p_new = jnp.exp(lg_new - m)
    l = p_new
    w_all = []
    for c in range(nc):
        p = jnp.exp(lg[c] - m)
        l = l + jnp.sum(p, axis=1, keepdims=True)
        w_all.append(jnp.concatenate([w_sb[c], p], axis=0))
    p_new_all = jnp.concatenate([jnp.zeros((n_sb, 1), F32), p_new], axis=0)
    inv_l = jnp.concatenate([jnp.ones((n_sb, 1), F32), 1.0 / l], axis=0)
    finish = lambda rows, acc: (acc + p_new_all[rows] * vn_ref[rows, :]) * inv_l[rows]
    _page_values(vbuf, slot, w_all, n_pages, ppc, finish, o_ref)


def _dec_specs(nseq, nh):
    const = lambda shape: pl.BlockSpec(shape, lambda b, pt: (0,) * len(shape), pipeline_mode=pl.Buffered(1))
    per_seq = pl.BlockSpec((None, nh, HEAD_DIM), lambda b, pt: (b, 0, 0))
    return const, per_seq


def _dec_even(page_table, qt, knt, vn, dbias, b0, tri, k_pool, v_pool, n_sb):
    nseq, n_pages = page_table.shape
    nh = qt.shape[0]
    const, per_seq = _dec_specs(nseq, nh)
    page = (nh, HEAD_DIM, PAGE_SIZE)
    grid_spec = pltpu.PrefetchScalarGridSpec(
        num_scalar_prefetch=1,
        grid=(nseq,),
        in_specs=[const(qt.shape), const(knt.shape), per_seq, const(dbias.shape), const(b0.shape), const(tri.shape),
                  pl.BlockSpec(memory_space=pl.ANY), pl.BlockSpec(memory_space=pl.ANY)],
        out_specs=per_seq,
        scratch_shapes=[pltpu.VMEM((2, n_pages) + page, F32), pltpu.VMEM((2, n_pages) + page, F32),
                        pltpu.SemaphoreType.DMA((2, n_pages)), pltpu.SemaphoreType.DMA((2, n_pages))],
    )
    return pl.pallas_call(
        functools.partial(_dec_even_kernel, n_pages=n_pages, n_sb=n_sb, chunk=MOBA_BLOCK),
        grid_spec=grid_spec,
        out_shape=jax.ShapeDtypeStruct((nseq, nh, HEAD_DIM), F32),
        compiler_params=_params("arbitrary"),
        name="decode_even",
    )(page_table, qt, knt, vn, dbias, b0, tri, k_pool, v_pool)


def _dec_odd_kernel(pt_ref, qt_ref, knt_ref, vn_ref, lfn_ref, tri_ref, kc_ref, vc_ref, lc_ref,
                    o_ref, kbuf, vbuf, lbuf, ksem, vsem, lsem, *, n_pages, chunk):
    slot = _gather_pipeline(pt_ref, n_pages, [(kc_ref, kbuf, ksem), (vc_ref, vbuf, vsem), (lc_ref, lbuf, lsem)])
    b = pl.program_id(0)
    ppc = chunk // PAGE_SIZE
    nc = n_pages // ppc

    qt = qt_ref[...]
    zs = _page_logits(kbuf, slot, _own_column(qt, b), n_pages, ppc)
    lg_new = _own_column(jnp.sum(qt * knt_ref[...], axis=1), b)
    tri = tri_ref[...]
    run = _own_column(lfn_ref[...], b)
    lg = [None] * nc
    for c in reversed(range(nc)):
        lf_c = jnp.concatenate([lbuf[slot, p] for p in range(c * ppc, (c + 1) * ppc)], axis=1)
        a, b3, d = _split3(lf_c)
        tail = _dot(a, tri) + _dot(b3, tri) + _dot(d, tri)
        lg[c] = zs[c] + tail + run
        run = run + tail[:, 0:1] + lf_c[:, 0:1]
    m = lg_new
    for c in range(nc):
        m = jnp.maximum(m, jnp.max(lg[c], axis=1, keepdims=True))
    p_new = jnp.exp(lg_new - m)
    l = p_new
    w_all = []
    for c in range(nc):
        p = jnp.exp(lg[c] - m)
        l = l + jnp.sum(p, axis=1, keepdims=True)
        w_all.append(p)
    finish = lambda rows, acc: (acc + p_new[rows] * vn_ref[rows, :]) / l[rows]
    _page_values(vbuf, slot, w_all, n_pages, ppc, finish, o_ref)


def _dec_odd(page_table, qt, knt, vn, lfn, tri, k_pool, v_pool, lf_pool):
    nseq, n_pages = page_table.shape
    nh = qt.shape[0]
    const, per_seq = _dec_specs(nseq, nh)
    page = (nh, HEAD_DIM, PAGE_SIZE)
    grid_spec = pltpu.PrefetchScalarGridSpec(
        num_scalar_prefetch=1,
        grid=(nseq,),
        in_specs=[const(qt.shape), const(knt.shape), per_seq, const(lfn.shape), const(tri.shape),
                  pl.BlockSpec(memory_space=pl.ANY), pl.BlockSpec(memory_space=pl.ANY),
                  pl.BlockSpec(memory_space=pl.ANY)],
        out_specs=per_seq,
        scratch_shapes=[pltpu.VMEM((2, n_pages) + page, F32), pltpu.VMEM((2, n_pages) + page, F32),
                        pltpu.VMEM((2, n_pages, nh, PAGE_SIZE), F32),
                        pltpu.SemaphoreType.DMA((2, n_pages)), pltpu.SemaphoreType.DMA((2, n_pages)),
                        pltpu.SemaphoreType.DMA((2, n_pages))],
    )
    return pl.pallas_call(
        functools.partial(_dec_odd_kernel, n_pages=n_pages, chunk=MOBA_BLOCK),
        grid_spec=grid_spec,
        out_shape=jax.ShapeDtypeStruct((nseq, nh, HEAD_DIM), F32),
        compiler_params=_params("arbitrary"),
        name="decode_odd",
    )(page_table, qt, knt, vn, lfn, tri, k_pool, v_pool, lf_pool)


def _tri(n, keep):
    r = lax.broadcasted_iota(jnp.int32, (n, n), 0)
    c = lax.broadcasted_iota(jnp.int32, (n, n), 1)
    return keep(r, c).astype(BF16)


def _token_major(xt, nh):
    g, _, seq = xt.shape
    return jnp.transpose(xt.reshape(g, nh, HEAD_DIM, seq), (0, 3, 1, 2))


def kernel(x_prompt, x_sample, cache_k_even, cache_v_even, cache_k_odd, cache_v_odd, cache_logf_odd, page_table, attn_norm, ffn_norm, w_in_even, w_out_even, q_norm_even, k_norm_even, rel_bias, w_in_odd, b_f_odd, w_out_odd, q_norm_odd, k_norm_odd, w_gate, w_up, w_down):
    bsz, seq, d = x_prompt.shape
    nseq, dec_seq, _ = x_sample.shape
    depth = attn_norm.shape[0]
    page = cache_k_even.shape[2]
    nh = cache_k_even.shape[3]
    dm = nh * HEAD_DIM
    n_mb = rel_bias.shape[1]
    n_sb = nh - n_mb
    half = n_sb * HEAD_DIM
    n_pages = page_table.shape[1]
    past = n_pages * page
    assert dec_seq == 1 and page == PAGE_SIZE and n_sb == n_mb and n_sb % PV_HEAD_GROUP == 0
    assert seq % MOBA_TILE == 0 and past % MOBA_BLOCK == 0 and half % MXU_DIM == 0 and MOBA_TILE % MOBA_BLOCK == 0

    tri_after = _tri(MOBA_BLOCK, lambda r, c: r > c)
    tri_incl = _tri(MOBA_BLOCK, lambda r, c: r <= c)
    seg = _tri(MXU_DIM, lambda r, c: (r // HEAD_DIM) == (c // HEAD_DIM)) * (1.0 / HEAD_DIM)

    pool_view = lambda cache: jnp.transpose(cache, (0, 2, 3, 1))
    heads_first = lambda a: jnp.transpose(a.reshape(nseq, nh, HEAD_DIM), (1, 2, 0))

    xp = x_prompt.reshape(bsz * seq, d)
    xs = x_sample.reshape(nseq, d)
    bias_tiles, dec_bias = _bias_tables(rel_bias, past, MOBA_TILE)
    dec_bias = dec_bias.reshape(n_mb, past)
    b0 = rel_bias[0].reshape(n_mb, 1)

    outs = {name: [] for name in ("ek_p", "ev_p", "ek_s", "ev_s", "ok_p", "ov_p", "of_p", "ok_s", "ov_s", "of_s")}
    r3 = lambda a: a.reshape(bsz, seq, a.shape[-1])
    sample_rows = lambda xt: jnp.transpose(xt[0]).reshape(nseq, nh, HEAD_DIM)
    for l in range(depth):
        i = l // 2
        g_attn = attn_norm[l].reshape(1, d)
        if l % 2 == 0:
            w = w_in_even[i]
            cols = lambda j: w[:, j * half:(j + 1) * half]
            wq = jnp.concatenate([cols(0), cols(3)], axis=1).astype(BF16)
            wkv = jnp.transpose(jnp.concatenate([cols(1), cols(4), cols(2), cols(5)], axis=1)).astype(BF16)
            gq = jnp.tile(q_norm_even[i], n_mb).reshape(1, half)
            gk = jnp.tile(k_norm_even[i], n_mb).reshape(half, 1)
            kt, vt, kbt, vbt, qb, qf = _proj_even(xp, g_attn, wq, wkv, gq, gk, seg, bsz)
            o_sb = _sb_attention(r3(qb), kbt, vbt, tri_after, n_sb // 2, SB_TILE)
            o_mb = _moba_attention(r3(qb), kbt, vbt, r3(qf), kt, bias_tiles, n_mb // 2, half // LANES)
            outs["ek_p"].append(_token_major(kt, nh))
            outs["ev_p"].append(_token_major(vt, nh))
            kt_s, vt_s, _, _, _, qf_s = _proj_even(xs, g_attn, wq, wkv, gq, gk, seg, 1)
            o_s = _dec_even(page_table, heads_first(qf_s), kt_s.reshape(nh, HEAD_DIM, nseq), sample_rows(vt_s),
                            dec_bias, b0, tri_after, pool_view(cache_k_even[i]), pool_view(cache_v_even[i]),
                            n_sb).reshape(nseq, dm)
            outs["ek_s"].append(_token_major(kt_s, nh).reshape(nseq, 1, nh, HEAD_DIM))
            outs["ev_s"].append(_token_major(vt_s, nh).reshape(nseq, 1, nh, HEAD_DIM))
            w_out = w_out_even[i].astype(BF16)
            o_p = (o_sb.reshape(bsz * seq, half), o_mb.reshape(bsz * seq, half), 0, 0)
        else:
            w = w_in_odd[i]
            wq = w[:, 0:dm].astype(BF16)
            wkv = jnp.pad(jnp.transpose(w[:, dm:]), ((0, LANES - nh), (0, 0))).astype(BF16)
            b_f = jnp.pad(b_f_odd[i], (0, LANES - nh)).reshape(LANES, 1)
            gq = jnp.tile(q_norm_odd[i], nh).reshape(1, dm)
            gk = jnp.tile(k_norm_odd[i], nh).reshape(dm, 1)
            kt, vt, kbt, vbt, qb, _, lft = _proj_odd(xp, g_attn, wq, wkv, gq, gk, b_f, seg, nh, bsz)
            c_t = _cumsum_lanes(lft, tri_incl)
            o = _fox_attention(r3(qb), kbt, vbt, c_t.reshape(bsz, nh // 2, 2, seq), FOX_TILE)
            outs["ok_p"].append(_token_major(kt, nh))
            outs["ov_p"].append(_token_major(vt, nh))
            outs["of_p"].append(jnp.transpose(lft, (0, 2, 1)))
            kt_s, vt_s, _, _, _, qf_s, lft_s = _proj_odd(xs, g_attn, wq, wkv, gq, gk, b_f, seg, nh, 1)
            o_s = _dec_odd(page_table, heads_first(qf_s), kt_s.reshape(nh, HEAD_DIM, nseq), sample_rows(vt_s),
                           lft_s[0], tri_after, pool_view(cache_k_odd[i]), pool_view(cache_v_odd[i]),
                           jnp.transpose(cache_logf_odd[i], (0, 2, 1))).reshape(nseq, dm)
            outs["ok_s"].append(_token_major(kt_s, nh).reshape(nseq, 1, nh, HEAD_DIM))
            outs["ov_s"].append(_token_major(vt_s, nh).reshape(nseq, 1, nh, HEAD_DIM))
            outs["of_s"].append(jnp.transpose(lft_s[0]).reshape(nseq, 1, nh))
            w_out = w_out_odd[i].astype(BF16)
            o2 = o.reshape(bsz * seq, dm)
            o_p = (o2, o2, 0, 1)
        g_ffn = ffn_norm[l].reshape(1, d)
        wg, wu, wd = w_gate[l].astype(BF16), w_up[l].astype(BF16), w_down[l].astype(BF16)
        xp = _post(xp, *o_p, w_out, g_ffn, wg, wu, wd)
        xs = _post(xs, o_s, o_s, 0, 1, w_out, g_ffn, wg, wu, wd)

    st = lambda name: jnp.stack(outs[name])
    return (xp.reshape(bsz, seq, d), xs.reshape(nseq, 1, d),
            st("ek_p"), st("ev_p"), st("ek_s"), st("ev_s"),
            st("ok_p"), st("ov_p"), st("of_p"), st("ok_s"), st("ov_s"), st("of_s"))
```

```python
import functools
import math

import jax
import jax.numpy as jnp
from jax import lax
from jax.experimental import pallas as pl
from jax.experimental.pallas import tpu as pltpu

F32 = jnp.float32
BF16 = jnp.bfloat16

HEAD_DIM = 64
EPS = 1e-6
SCALE = HEAD_DIM ** -0.5
MOBA_BLOCK = 256
MOBA_TOPK = 3
PAGE_SIZE = 128
REL_BUCKETS = 32
REL_MAX_DIST = 128
NEG = -1e30
LANES = 128
MXU_DIM = 256
VMEM_LIMIT = 56 * 1024 * 1024
ROW_TILE = 256
SB_TILE = 512
MOBA_TILE = 512
FOX_TILE = 512
PV_HEAD_GROUP = 4

NT = (((1,), (1,)), ((), ()))


def _bucket_thresholds():
    max_exact = REL_BUCKETS // 2
    def bucket(d):
        return min(max_exact + int(math.log(d / max_exact) / math.log(REL_MAX_DIST / max_exact)
                                   * (REL_BUCKETS - max_exact)), REL_BUCKETS - 1)
    return [min(d for d in range(max_exact, REL_MAX_DIST + 1) if bucket(d) >= b)
            for b in range(max_exact + 1, REL_BUCKETS)]


_BUCKET_THRESHOLDS = _bucket_thresholds()


def _params(*sem):
    return pltpu.CompilerParams(dimension_semantics=sem, vmem_limit_bytes=VMEM_LIMIT)


def _const_spec(shape):
    return pl.BlockSpec(shape, lambda *_: (0,) * len(shape), pipeline_mode=pl.Buffered(1))


def _log_sigmoid_parts(z, small_relative=True):
    e = jnp.exp(-jnp.abs(z))
    sp = jnp.log1p(e) if small_relative else jnp.log(1.0 + e)
    return jnp.minimum(z, 0.0) - sp, jnp.minimum(-z, 0.0) - sp


def _split2(x):
    hi = x.astype(BF16)
    lo = (x - hi.astype(F32)).astype(BF16)
    return hi, lo


def _split3(x):
    a = x.astype(BF16)
    r = x - a.astype(F32)
    b = r.astype(BF16)
    c = (r - b.astype(F32)).astype(BF16)
    return a, b, c


def _dot(a, b):
    return jnp.dot(a, b, preferred_element_type=F32)


def _dot_nt(a, b):
    return lax.dot_general(a, b, NT, preferred_element_type=F32)


def _rms(x, g):
    ms = jnp.mean(x * x, axis=-1, keepdims=True)
    return x * lax.rsqrt(ms + EPS) * g


def _head_rms(t, gain, seg):
    outs = []
    for c in range(t.shape[1] // MXU_DIM):
        tc = t[:, c * MXU_DIM:(c + 1) * MXU_DIM]
        hi, lo = _split2(tc * tc)
        ms = _dot(hi, seg) + _dot(lo, seg)
        outs.append(tc * lax.rsqrt(ms + EPS) * gain[:, c * MXU_DIM:(c + 1) * MXU_DIM])
    return jnp.concatenate(outs, axis=1)


def _head_rms_t(t, gain, seg):
    outs = []
    for c in range(t.shape[0] // MXU_DIM):
        tc = t[c * MXU_DIM:(c + 1) * MXU_DIM]
        hi, lo = _split2(tc * tc)
        ms = _dot(seg, hi) + _dot(seg, lo)
        outs.append(tc * lax.rsqrt(ms + EPS) * gain[c * MXU_DIM:(c + 1) * MXU_DIM])
    return jnp.concatenate(outs, axis=0)


def _proj_even_kernel(x_ref, g_ref, wq_ref, wkv_ref, gq_ref, gk_ref, seg_ref,
                      kt_ref, vt_ref, kbt_ref, vbt_ref, qb_ref, qf_ref, *, half):
    h = _rms(x_ref[...], g_ref[...]).astype(BF16)
    seg = seg_ref[...]
    uq = _dot(h, wq_ref[...])
    q = jnp.concatenate([uq[:, 0:half], _head_rms(uq[:, half:], gq_ref[...], seg)], axis=1) * SCALE
    qf_ref[...] = q
    qb_ref[...] = q.astype(BF16)
    ukv = _dot_nt(wkv_ref[...], h)
    k = jnp.concatenate([ukv[0:half], _head_rms_t(ukv[half:2 * half], gk_ref[...], seg)], axis=0)
    v = ukv[2 * half:]
    kt_ref[...] = k
    vt_ref[...] = v
    kbt_ref[...] = k.astype(BF16)
    vbt_ref[...] = v.astype(BF16)


def _feature_major_specs(groups, seq, tm, feats):
    nt = seq // tm
    spec = pl.BlockSpec((None, feats, tm), lambda i: (i // nt, 0, i % nt))
    return spec, lambda dt: jax.ShapeDtypeStruct((groups, feats, seq), dt)


def _proj_even(x, g, wq, wkv, gq, gk_col, seg, groups):
    rows, d = x.shape
    dm = wq.shape[1]
    half = dm // 2
    seq = rows // groups
    tm = min(ROW_TILE, seq)
    row = lambda w: pl.BlockSpec((tm, w), lambda i: (i, 0))
    fm, fm_shape = _feature_major_specs(groups, seq, tm, dm)
    return pl.pallas_call(
        functools.partial(_proj_even_kernel, half=half),
        grid=(rows // tm,),
        in_specs=[row(d), _const_spec((1, d)), _const_spec(wq.shape), _const_spec(wkv.shape),
                  _const_spec((1, half)), _const_spec((half, 1)), _const_spec(seg.shape)],
        out_specs=[fm, fm, fm, fm, row(dm), row(dm)],
        out_shape=[fm_shape(F32), fm_shape(F32), fm_shape(BF16), fm_shape(BF16),
                   jax.ShapeDtypeStruct((rows, dm), BF16), jax.ShapeDtypeStruct((rows, dm), F32)],
        compiler_params=_params("parallel"),
        name="proj_even",
    )(x, g, wq, wkv, gq, gk_col, seg)


def _proj_odd_kernel(x_ref, g_ref, wq_ref, wkv_ref, gq_ref, gk_ref, bf_ref, seg_ref,
                     kt_ref, vt_ref, kbt_ref, vbt_ref, qb_ref, qf_ref, lft_ref, *, dm, nh):
    h = _rms(x_ref[...], g_ref[...]).astype(BF16)
    seg = seg_ref[...]
    q = _head_rms(_dot(h, wq_ref[...]), gq_ref[...], seg) * SCALE
    qf_ref[...] = q
    qb_ref[...] = q.astype(BF16)
    ukv = _dot_nt(wkv_ref[...], h)
    k = _head_rms_t(ukv[0:dm], gk_ref[...], seg)
    v = ukv[dm:2 * dm]
    kt_ref[...] = k
    vt_ref[...] = v
    kbt_ref[...] = k.astype(BF16)
    vbt_ref[...] = v.astype(BF16)
    lf, _ = _log_sigmoid_parts(ukv[2 * dm:] + bf_ref[...])
    lft_ref[...] = lf[0:nh]


def _proj_odd(x, g, wq, wkv, gq, gk_col, bf_col, seg, nh, groups):
    rows, d = x.shape
    dm = wq.shape[1]
    seq = rows // groups
    tm = min(ROW_TILE, seq)
    row = lambda w: pl.BlockSpec((tm, w), lambda i: (i, 0))
    fm, fm_shape = _feature_major_specs(groups, seq, tm, dm)
    lfm, lfm_shape = _feature_major_specs(groups, seq, tm, nh)
    return pl.pallas_call(
        functools.partial(_proj_odd_kernel, dm=dm, nh=nh),
        grid=(rows // tm,),
        in_specs=[row(d), _const_spec((1, d)), _const_spec(wq.shape), _const_spec(wkv.shape),
                  _const_spec((1, dm)), _const_spec((dm, 1)), _const_spec((LANES, 1)), _const_spec(seg.shape)],
        out_specs=[fm, fm, fm, fm, row(dm), row(dm), lfm],
        out_shape=[fm_shape(F32), fm_shape(F32), fm_shape(BF16), fm_shape(BF16),
                   jax.ShapeDtypeStruct((rows, dm), BF16), jax.ShapeDtypeStruct((rows, dm), F32), lfm_shape(F32)],
        compiler_params=_params("parallel"),
        name="proj_odd",
    )(x, g, wq, wkv, gq, gk_col, bf_col, seg)


def _post_kernel(x_ref, oa_ref, ob_ref, wo_ref, g_ref, wg_ref, wu_ref, wd_ref, y_ref):
    ha = oa_ref.shape[1]
    x1 = (x_ref[...] + _dot(oa_ref[...].astype(BF16), wo_ref[0:ha, :])
          + _dot(ob_ref[...].astype(BF16), wo_ref[ha:, :]))
    h = _rms(x1, g_ref[...]).astype(BF16)
    gate = _dot(h, wg_ref[...])
    up = _dot(h, wu_ref[...])
    act = (gate * (1.0 / (1.0 + jnp.exp(-gate))) * up).astype(BF16)
    y_ref[...] = x1 + _dot(act, wd_ref[...])


def _post(x, oa, ob, oa_col, ob_col, wo, g, wg, wu, wd):
    rows, d = x.shape
    half = wo.shape[0] // 2
    tm = min(ROW_TILE, rows)
    return pl.pallas_call(
        _post_kernel,
        grid=(rows // tm,),
        in_specs=[pl.BlockSpec((tm, d), lambda i: (i, 0)),
                  pl.BlockSpec((tm, half), lambda i: (i, oa_col)),
                  pl.BlockSpec((tm, half), lambda i: (i, ob_col)),
                  _const_spec(wo.shape), _const_spec((1, d)), _const_spec(wg.shape),
                  _const_spec(wu.shape), _const_spec(wd.shape)],
        out_specs=pl.BlockSpec((tm, d), lambda i: (i, 0)),
        out_shape=jax.ShapeDtypeStruct((rows, d), F32),
        compiler_params=_params("parallel"),
        name="post_attn",
    )(x, oa, ob, wo, g, wg, wu, wd)


def _rel_bias_of(dist, rel_ref, h):
    d = jnp.maximum(dist, 0)
    big = jnp.full(d.shape, REL_BUCKETS // 2, jnp.int32)
    for thr in _BUCKET_THRESHOLDS:
        big = big + jnp.where(d >= thr, 1, 0)
    bucket = jnp.where(d < REL_BUCKETS // 2, d, big)
    out = jnp.full(d.shape, rel_ref[REL_BUCKETS - 1, h], F32)
    for b in range(REL_BUCKETS - 1):
        out = jnp.where(bucket == b, rel_ref[b, h], out)
    return out


def _bias_kernel(rel_ref, tiles_ref, dec_ref, *, blk, past_len):
    h = pl.program_id(0)
    row = lax.broadcasted_iota(jnp.int32, (blk, blk), 0)
    col = lax.broadcasted_iota(jnp.int32, (blk, blk), 1)
    tiles_ref[0] = _rel_bias_of(row - col, rel_ref, h)
    tiles_ref[1] = _rel_bias_of(row - col + blk, rel_ref, h)
    tiles_ref[2] = jnp.full((blk, blk), rel_ref[REL_BUCKETS - 1, h], F32)
    pos = lax.broadcasted_iota(jnp.int32, (1, past_len), 1)
    dec_ref[...] = _rel_bias_of(past_len - pos, rel_ref, h)


def _bias_tables(rel_bias, past_len, blk):
    assert blk + 1 >= REL_MAX_DIST, "tile 2 must be all in the last bucket"
    nh = rel_bias.shape[1]
    return pl.pallas_call(
        functools.partial(_bias_kernel, blk=blk, past_len=past_len),
        grid=(nh,),
        in_specs=[pl.BlockSpec(memory_space=pltpu.SMEM)],
        out_specs=[pl.BlockSpec((None, 3, blk, blk), lambda h: (h, 0, 0, 0)),
                   pl.BlockSpec((None, 1, past_len), lambda h: (h, 0, 0))],
        out_shape=[jax.ShapeDtypeStruct((nh, 3, blk, blk), F32),
                   jax.ShapeDtypeStruct((nh, 1, past_len), F32)],
        compiler_params=_params("parallel"),
        name="rel_bias_tables",
    )(rel_bias)


def _pair_queries(q2):
    lane = lax.broadcasted_iota(jnp.int32, q2.shape, 1)
    qf = q2.astype(F32)
    return [jnp.where(lane < HEAD_DIM, qf, 0.0).astype(q2.dtype), jnp.where(lane >= HEAD_DIM, qf, 0.0).astype(q2.dtype)]


def _pair_merge(a0, a1):
    lane = lax.broadcasted_iota(jnp.int32, a0.shape, 1)
    return jnp.where(lane < HEAD_DIM, a0, a1)


def _kv_tile(k_ref, v_ref, j, tk):
    start = pl.multiple_of(j * tk, tk)
    return k_ref[:, pl.ds(start, tk)], v_ref[:, pl.ds(start, tk)], start


def _sb_kernel(q_ref, k_ref, v_ref, tri_ref, o_ref, *, tq, tk):
    i = pl.program_id(2)
    qh = _pair_queries(q_ref[...])
    tri = tri_ref[...]
    row = lax.broadcasted_iota(jnp.int32, (tq, tk), 0)
    col = lax.broadcasted_iota(jnp.int32, (tq, tk), 1)

    def block(start, carry, mask):
        kt = k_ref[:, pl.ds(start, tk)]
        vt = v_ref[:, pl.ds(start, tk)]
        new = []
        for h in range(2):
            run, acc = carry[h]
            lb, l1 = _log_sigmoid_parts(_dot(qh[h], kt), small_relative=False)
            if mask is not None:
                l1 = jnp.where(mask, l1, 0.0)
            tail = _dot(l1.astype(BF16), tri)
            w = jnp.exp(lb + tail + run)
            if mask is not None:
                w = jnp.where(mask, w, 0.0)
            acc = acc + _dot_nt(w.astype(BF16), vt)
            run = run + tail[:, 0:1] + l1[:, 0:1]
            new.append((run, acc))
        return tuple(new)

    def tile(j, carry, diag):
        for blk in reversed(range(tq // tk)):
            mask = (col + blk * tk < row) if diag else None
            carry = block(pl.multiple_of(j * tq + blk * tk, tk), carry, mask)
        return carry

    init = tuple((jnp.zeros((tq, 1), F32), jnp.zeros((tq, LANES), F32)) for _ in range(2))
    carry = tile(i, init, True)
    carry = lax.fori_loop(0, i, lambda t, c: tile(i - 1 - t, c, False), carry)
    o_ref[...] = _pair_merge(carry[0][1], carry[1][1]).astype(o_ref.dtype)


def _pair_specs(t, tq, col0):
    q = pl.BlockSpec((None, tq, LANES), lambda bi, p, i, *_: (bi, i, col0 + p))
    kv = pl.BlockSpec((None, LANES, t), lambda bi, p, i, *_: (bi, col0 + p, 0))
    o = pl.BlockSpec((None, tq, LANES), lambda bi, p, i, *_: (bi, i, p))
    return q, kv, o


def _sb_host(qb, kbt, vbt, tri, n_pairs, tq):
    b, t, _ = qb.shape
    q_spec, kv_spec, o_spec = _pair_specs(t, tq, 0)
    return dict(grid=(b, n_pairs, t // tq), host=functools.partial(_sb_kernel, tq=tq, tk=tri.shape[0]),
                host_specs=[q_spec, kv_spec, kv_spec, _const_spec(tri.shape)], host_out_spec=o_spec,
                host_out_shape=jax.ShapeDtypeStruct((b, t, n_pairs * LANES), BF16), host_args=(qb, kbt, vbt, tri),
                host_scratch=[])


def _softmax_tile(s, vt, state):
    m, l, acc = state
    m_new = jnp.maximum(m, jnp.max(s, axis=1, keepdims=True))
    alpha = jnp.exp(m - m_new)
    p = jnp.exp(s - m_new)
    l = alpha * l + jnp.sum(p, axis=1, keepdims=True)
    acc = alpha * acc + _dot_nt(p.astype(BF16), vt)
    return m_new, l, acc


def _softmax_init(tq):
    return (jnp.full((tq, 1), NEG, F32), jnp.zeros((tq, 1), F32), jnp.zeros((tq, LANES), F32))


def _softmax_finish(states, o_ref):
    outs = [acc / l for (_, l, acc) in states]
    o_ref[...] = _pair_merge(outs[0], outs[1]).astype(o_ref.dtype)


def _fox_kernel(q_ref, k_ref, v_ref, ck_ref, o_ref, *, tq):
    i = pl.program_id(2)
    qh = _pair_queries(q_ref[...])
    row = lax.broadcasted_iota(jnp.int32, (tq, tq), 0)
    col = lax.broadcasted_iota(jnp.int32, (tq, tq), 1)
    causal = col <= row

    def tile(j, states, diag):
        kt, vt, start = _kv_tile(k_ref, v_ref, j, tq)
        new = []
        for h in range(2):
            s = _dot(qh[h], kt) - ck_ref[h:h + 1, pl.ds(start, tq)]
            if diag:
                s = jnp.where(causal, s, NEG)
            new.append(_softmax_tile(s, vt, states[h]))
        return tuple(new)

    states = tile(i, (_softmax_init(tq), _softmax_init(tq)), True)
    states = lax.fori_loop(0, i, lambda j, st: tile(j, st, False), states)
    _softmax_finish(states, o_ref)


def _fox_host(qb, kbt, vbt, ck, tq):
    b, t, dm = qb.shape
    q_spec, kv_spec, o_spec = _pair_specs(t, tq, 0)
    return dict(grid=(b, dm // LANES, t // tq), host=functools.partial(_fox_kernel, tq=tq),
                host_specs=[q_spec, kv_spec, kv_spec,
                            pl.BlockSpec((None, None, 2, t), lambda bi, p, i, *_: (bi, p, 0, 0))],
                host_out_spec=o_spec, host_out_shape=jax.ShapeDtypeStruct((b, t, dm), BF16),
                host_args=(qb, kbt, vbt, ck), host_scratch=[])


def _cumsum_kernel(x_ref, tri_ref, o_ref, *, chunk):
    tri = tri_ref[...]
    carry = jnp.zeros((x_ref.shape[0], 1), F32)
    for c in range(x_ref.shape[1] // chunk):
        a, b, d = _split3(x_ref[:, c * chunk:(c + 1) * chunk])
        cs = _dot(a, tri) + _dot(b, tri) + _dot(d, tri) + carry
        o_ref[:, c * chunk:(c + 1) * chunk] = cs
        carry = cs[:, chunk - 1:chunk]


def _cumsum_lanes(x, tri_incl):
    b, nh, t = x.shape
    return pl.pallas_call(
        functools.partial(_cumsum_kernel, chunk=tri_incl.shape[0]),
        grid=(b,),
        in_specs=[pl.BlockSpec((None, nh, t), lambda bi: (bi, 0, 0)), _const_spec(tri_incl.shape)],
        out_specs=pl.BlockSpec((None, nh, t), lambda bi: (bi, 0, 0)),
        out_shape=jax.ShapeDtypeStruct((b, nh, t), F32),
        compiler_params=_params("parallel"),
        name="logf_cumsum",
    )(x, tri_incl)


def _top_blocks(scores, n_valid, n_blocks, topk):
    blk = lax.broadcasted_iota(jnp.int32, scores.shape, 0)
    rank = jnp.zeros(scores.shape, F32)
    for m in range(n_blocks):
        sm = scores[m:m + 1, :]
        beats = (sm > scores) | ((sm == scores) & (m < blk))
        rank = rank + jnp.where(beats, jnp.where(m < n_valid, 1.0, 0.0), 0.0)
    return jnp.where(((rank < topk) & (blk < n_valid)) | (blk == n_valid), 1.0, 0.0)


def _moba_kernel(q_ref, k_ref, v_ref, qf_ref, kf_ref, bias_ref, o_ref, km_ref, kaug_ref, *, tq, nb):
    i = pl.program_id(2)
    t = k_ref.shape[1]
    blk_shift = MOBA_BLOCK.bit_length() - 1

    @pl.when(i == 0)
    def _():
        lane = lax.broadcasted_iota(jnp.int32, (LANES, LANES), 1)
        sums = jnp.zeros((LANES, LANES), F32)
        for n in range(nb):
            col = jnp.sum(kf_ref[:, n * MOBA_BLOCK:(n + 1) * MOBA_BLOCK], axis=1, keepdims=True)
            sums = jnp.where(lane == n, col, sums)
        means = jnp.transpose(sums)[0:nb] * (1.0 / MOBA_BLOCK)
        lane = lane[0:nb]
        km_ref[0:nb, :] = jnp.where(lane < HEAD_DIM, means, 0.0)
        km_ref[nb:, :] = jnp.where(lane >= HEAD_DIM, means, 0.0)
        kaug_ref[0:LANES, :] = k_ref[...]
        c = lax.broadcasted_iota(jnp.int32, (LANES, t), 0)
        s = lax.broadcasted_iota(jnp.int32, (LANES, t), 1)
        ind = (c < 2 * nb) & (jnp.where(c >= nb, c - nb, c) == jnp.right_shift(s, blk_shift))
        kaug_ref[LANES:, :] = jnp.where(ind, 1.0, 0.0).astype(BF16)

    ka, kb2 = _split2(km_ref[...])
    qa, qb2 = _split2(qf_ref[...])
    scores = _dot_nt(ka, qa) + _dot_nt(kb2, qa) + _dot_nt(ka, qb2)
    pos = lax.broadcasted_iota(jnp.int32, (1, tq), 1)
    own = i * (tq // MOBA_BLOCK) + jnp.right_shift(pos, blk_shift)
    sel = jnp.concatenate([_top_blocks(scores[h * nb:(h + 1) * nb], own, nb, MOBA_TOPK) for h in range(2)]
                          + [jnp.zeros((LANES - 2 * nb, tq), F32)], axis=0)
    sel_t = jnp.transpose(sel)
    colq = lax.broadcasted_iota(jnp.int32, (tq, LANES), 1)
    qh = _pair_queries(q_ref[...])
    qaug = []
    for h in range(2):
        mine = (colq >= h * nb) & (colq < (h + 1) * nb)
        neg = jnp.where(mine, jnp.where(sel_t < 0.5, NEG, 0.0), 0.0)
        qaug.append(jnp.concatenate([qh[h], neg.astype(BF16)], axis=1))

    row = lax.broadcasted_iota(jnp.int32, (tq, tq), 0)
    col = lax.broadcasted_iota(jnp.int32, (tq, tq), 1)
    causal = col <= row

    def tile(j, states, diag):
        start = pl.multiple_of(j * tq, tq)
        kt = kaug_ref[:, pl.ds(start, tq)]
        vt = v_ref[:, pl.ds(start, tq)]
        new = []
        for h in range(2):
            s = _dot(qaug[h], kt) + bias_ref[h, jnp.minimum(i - j, 2)]
            if diag:
                s = jnp.where(causal, s, NEG)
            new.append(_softmax_tile(s, vt, states[h]))
        return tuple(new)

    states = tile(i, (_softmax_init(tq), _softmax_init(tq)), True)
    states = lax.fori_loop(0, i, lambda j, st: tile(j, st, False), states)
    _softmax_finish(states, o_ref)


def _moba_attention(qb, kbt, vbt, qf, kt, bias_tiles, n_pairs, col0):
    b, t, _ = qb.shape
    tq = bias_tiles.shape[-1]
    nb = t // MOBA_BLOCK
    q_spec, kv_spec, o_spec = _pair_specs(t, tq, col0)
    return pl.pallas_call(
        functools.partial(_moba_kernel, tq=tq, nb=nb),
        grid=(b, n_pairs, t // tq),
        in_specs=[q_spec, kv_spec, kv_spec, q_spec, kv_spec,
                  pl.BlockSpec((2, 3, tq, tq), lambda bi, p, i: (p, 0, 0, 0))],
        out_specs=o_spec,
        out_shape=jax.ShapeDtypeStruct((b, t, n_pairs * LANES), BF16),
        scratch_shapes=[pltpu.VMEM((2 * nb, LANES), F32), pltpu.VMEM((2 * LANES, t), BF16)],
        compiler_params=_params("parallel", "parallel", "arbitrary"),
        name="moba_attention",
    )(qb, kbt, vbt, qf, kt, bias_tiles)


def _page_copies(pt_ref, b, slot, n_pages, pools_bufs_sems):
    cps = []
    for pool, buf, sem in pools_bufs_sems:
        for p in range(n_pages):
            cps.append(pltpu.make_async_copy(pool.at[pt_ref[b, p]], buf.at[slot, p], sem.at[slot, p]))
    return cps


def _gather_pipeline(pt_ref, b, nseq, n_pages, pools_bufs_sems):
    slot = b % 2

    @pl.when(b == 0)
    def _():
        for cp in _page_copies(pt_ref, b, slot, n_pages, pools_bufs_sems):
            cp.start()

    @pl.when(b + 1 < nseq)
    def _():
        for cp in _page_copies(pt_ref, b + 1, 1 - slot, n_pages, pools_bufs_sems):
            cp.start()

    for cp in _page_copies(pt_ref, b, slot, n_pages, pools_bufs_sems):
        cp.wait()
    return slot


def _own_column(x, b):
    lane = lax.broadcasted_iota(jnp.int32, x.shape, x.ndim - 1)
    return jnp.sum(jnp.where(lane == b, x, 0.0), axis=x.ndim - 1, keepdims=True)


def _page_logits(kbuf, slot, q_col, n_pages, pages_per_chunk):
    q_bc = jnp.broadcast_to(q_col, q_col.shape[:2] + (PAGE_SIZE,))
    z = [jnp.sum(kbuf[slot, p] * q_bc, axis=1) for p in range(n_pages)]
    return [jnp.concatenate(z[c:c + pages_per_chunk], axis=1) for c in range(0, n_pages, pages_per_chunk)]


def _page_values(vbuf, slot, w_chunks, n_pages, pages_per_chunk, finish, o_ref):
    nh = w_chunks[0].shape[0]
    for g in range(0, nh, PV_HEAD_GROUP):
        rows = slice(g, g + PV_HEAD_GROUP)
        acc = jnp.zeros((PV_HEAD_GROUP, HEAD_DIM, PAGE_SIZE), F32)
        for p in range(n_pages):
            c, r = divmod(p, pages_per_chunk)
            w = w_chunks[c][rows, r * PAGE_SIZE:(r + 1) * PAGE_SIZE]
            acc = acc + vbuf[slot, p, rows] * w[:, None, :]
        o_ref[rows, :] = finish(rows, jnp.sum(acc, axis=2))


def _dec_even_kernel(b, pt_ref, qt_ref, knt_ref, dbias_ref, b0_ref, tri_ref, vn_ref, kc_ref, vc_ref,
                     o_ref, kbuf, vbuf, ksem, vsem, *, n_pages, n_sb, chunk):
    slot = _gather_pipeline(pt_ref, b, qt_ref.shape[2], n_pages, [(kc_ref, kbuf, ksem), (vc_ref, vbuf, vsem)])
    ppc = chunk // PAGE_SIZE
    nc = n_pages // ppc
    nh = qt_ref.shape[0]
    n_mb = nh - n_sb

    qt = qt_ref[...]
    zs = _page_logits(kbuf, slot, _own_column(qt, b), n_pages, ppc)
    z_new = _own_column(jnp.sum(qt * knt_ref[...], axis=1), b)

    tri = tri_ref[...]
    run = jnp.zeros((n_sb, 1), F32)
    w_sb = [None] * nc
    for c in reversed(range(nc)):
        lb, l1 = _log_sigmoid_parts(zs[c][0:n_sb])
        hi, lo = _split2(l1)
        tail = _dot(hi, tri) + _dot(lo, tri)
        w_sb[c] = jnp.exp(lb + tail + run)
        run = run + tail[:, 0:1] + l1[:, 0:1]

    sc = [jnp.sum(zs[n][n_sb:], axis=1, keepdims=True) * (1.0 / MOBA_BLOCK) for n in range(nc)]
    lg = []
    for n in range(nc):
        rank = jnp.zeros((n_mb, 1), F32)
        for m in range(nc):
            if m != n:
                beats = (sc[m] > sc[n]) | ((sc[m] == sc[n]) & (m < n))
                rank = rank + jnp.where(beats, 1.0, 0.0)
        mask = jnp.where(rank < MOBA_TOPK, 0.0, NEG)
        lg.append(zs[n][n_sb:] + dbias_ref[:, n * chunk:(n + 1) * chunk] + mask)
    lg_new = z_new[n_sb:] + b0_ref[...]
    m = lg_new
    for n in range(nc):
        m = jnp.maximum(m, jnp.max(lg[n], axis=1, keepdims=True))
    p_new = jnp.exp(lg_new - m)
    l = p_new
    w_all = []
    for c in range(nc):
        p = jnp.exp(lg[c] - m)
        l = l + jnp.sum(p, axis=1, keepdims=True)
        w_all.append(jnp.concatenate([w_sb[c], p], axis=0))
    p_new_all = jnp.concatenate([jnp.zeros((n_sb, 1), F32), p_new], axis=0)
    inv_l = jnp.concatenate([jnp.ones((n_sb, 1), F32), 1.0 / l], axis=0)
    finish = lambda rows, acc: (acc + p_new_all[rows] * vn_ref[rows, :]) * inv_l[rows]
    _page_values(vbuf, slot, w_all, n_pages, ppc, finish, o_ref)


def _dec_odd_kernel(b, pt_ref, qt_ref, knt_ref, lfn_ref, tri_ref, vn_ref, kc_ref, vc_ref, lc_ref,
                    o_ref, kbuf, vbuf, lbuf, ksem, vsem, lsem, *, n_pages, chunk):
    slot = _gather_pipeline(pt_ref, b, qt_ref.shape[2], n_pages,
                            [(kc_ref, kbuf, ksem), (vc_ref, vbuf, vsem), (lc_ref, lbuf, lsem)])
    ppc = chunk // PAGE_SIZE
    nc = n_pages // ppc

    qt = qt_ref[...]
    zs = _page_logits(kbuf, slot, _own_column(qt, b), n_pages, ppc)
    lg_new = _own_column(jnp.sum(qt * knt_ref[...], axis=1), b)
    tri = tri_ref[...]
    run = _own_column(lfn_ref[...], b)
    lg = [None] * nc
    for c in reversed(range(nc)):
        lf_c = jnp.concatenate([lbuf[slot, p] for p in range(c * ppc, (c + 1) * ppc)], axis=1)
        a, b3, d = _split3(lf_c)
        tail = _dot(a, tri) + _dot(b3, tri) + _dot(d, tri)
        lg[c] = zs[c] + tail + run
        run = run + tail[:, 0:1] + lf_c[:, 0:1]
    m = lg_new
    for c in range(nc):
        m = jnp.maximum(m, jnp.max(lg[c], axis=1, keepdims=True))
    p_new = jnp.exp(lg_new - m)
    l = p_new
    w_all = []
    for c in range(nc):
        p = jnp.exp(lg[c] - m)
        l = l + jnp.sum(p, axis=1, keepdims=True)
        w_all.append(p)
    finish = lambda rows, acc: (acc + p_new[rows] * vn_ref[rows, :]) / l[rows]
    _page_values(vbuf, slot, w_all, n_pages, ppc, finish, o_ref)


def _page_scratch(n_pages, nh, with_logf):
    page = (2, n_pages, nh, HEAD_DIM, PAGE_SIZE)
    bufs = [pltpu.VMEM(page, F32), pltpu.VMEM(page, F32)]
    if with_logf:
        bufs.append(pltpu.VMEM((2, n_pages, nh, PAGE_SIZE), F32))
    return bufs + [pltpu.SemaphoreType.DMA((2, n_pages)) for _ in bufs]


def _attend_and_decode_kernel(pt_ref, *refs, host, n_host_in, n_host_scratch, dec, n_dec_in, steps_per_seq):
    host_in = refs[:n_host_in]
    dec_in = refs[n_host_in:n_host_in + n_dec_in]
    host_out, dec_out = refs[n_host_in + n_dec_in:n_host_in + n_dec_in + 2]
    scratch = refs[n_host_in + n_dec_in + 2:]
    host_scratch, dec_scratch = scratch[:n_host_scratch], scratch[n_host_scratch:]
    step = ((pl.program_id(0) * pl.num_programs(1) + pl.program_id(1)) * pl.num_programs(2)
            + pl.program_id(2))

    @pl.when(step % steps_per_seq == 0)
    def _():
        dec(step // steps_per_seq, pt_ref, *dec_in, dec_out, *dec_scratch)

    host(*host_in, host_out, *host_scratch)


def _attend_and_decode(name, page_table, dec, dec_consts, vn, pools, scratch, *, grid, host, host_specs,
                       host_out_spec, host_out_shape, host_args, host_scratch):
    nseq = page_table.shape[0]
    nh = vn.shape[1]
    steps = math.prod(grid)
    assert steps % nseq == 0
    spp = steps // nseq
    per_seq = pl.BlockSpec((None, nh, HEAD_DIM),
                           lambda bi, p, i, pt: (((bi * grid[1] + p) * grid[2] + i) // spp, 0, 0))
    dec_specs = ([_const_spec(a.shape) for a in dec_consts] + [per_seq]
                 + [pl.BlockSpec(memory_space=pl.ANY) for _ in pools])
    grid_spec = pltpu.PrefetchScalarGridSpec(
        num_scalar_prefetch=1, grid=grid, in_specs=list(host_specs) + dec_specs,
        out_specs=[host_out_spec, per_seq], scratch_shapes=list(host_scratch) + list(scratch))
    return pl.pallas_call(
        functools.partial(_attend_and_decode_kernel, host=host, n_host_in=len(host_args),
                          n_host_scratch=len(host_scratch), dec=dec,
                          n_dec_in=len(dec_specs), steps_per_seq=spp),
        grid_spec=grid_spec,
        out_shape=[host_out_shape, jax.ShapeDtypeStruct((nseq, nh, HEAD_DIM), F32)],
        compiler_params=_params("arbitrary", "arbitrary", "arbitrary"),
        name=name,
    )(page_table, *host_args, *dec_consts, vn, *pools)


def _tri(n, keep):
    r = lax.broadcasted_iota(jnp.int32, (n, n), 0)
    c = lax.broadcasted_iota(jnp.int32, (n, n), 1)
    return keep(r, c).astype(BF16)


def _token_major(xt, nh):
    g, _, seq = xt.shape
    return jnp.transpose(xt.reshape(g, nh, HEAD_DIM, seq), (0, 3, 1, 2))


def kernel(x_prompt, x_sample, cache_k_even, cache_v_even, cache_k_odd, cache_v_odd, cache_logf_odd, page_table, attn_norm, ffn_norm, w_in_even, w_out_even, q_norm_even, k_norm_even, rel_bias, w_in_odd, b_f_odd, w_out_odd, q_norm_odd, k_norm_odd, w_gate, w_up, w_down):
    bsz, seq, d = x_prompt.shape
    nseq, dec_seq, _ = x_sample.shape
    depth = attn_norm.shape[0]
    page = cache_k_even.shape[2]
    nh = cache_k_even.shape[3]
    dm = nh * HEAD_DIM
    n_mb = rel_bias.shape[1]
    n_sb = nh - n_mb
    half = n_sb * HEAD_DIM
    n_pages = page_table.shape[1]
    past = n_pages * page
    assert dec_seq == 1 and page == PAGE_SIZE and n_sb == n_mb and n_sb % PV_HEAD_GROUP == 0
    assert seq % MOBA_TILE == 0 and past % MOBA_BLOCK == 0 and half % MXU_DIM == 0 and MOBA_TILE % MOBA_BLOCK == 0

    tri_after = _tri(MOBA_BLOCK, lambda r, c: r > c)
    tri_incl = _tri(MOBA_BLOCK, lambda r, c: r <= c)
    seg = _tri(MXU_DIM, lambda r, c: (r // HEAD_DIM) == (c // HEAD_DIM)) * (1.0 / HEAD_DIM)

    pool_view = lambda cache: jnp.transpose(cache, (0, 2, 3, 1))
    heads_first = lambda a: jnp.transpose(a.reshape(nseq, nh, HEAD_DIM), (1, 2, 0))

    xp = x_prompt.reshape(bsz * seq, d)
    xs = x_sample.reshape(nseq, d)
    bias_tiles, dec_bias = _bias_tables(rel_bias, past, MOBA_TILE)
    dec_bias = dec_bias.reshape(n_mb, past)
    b0 = rel_bias[0].reshape(n_mb, 1)

    outs = {name: [] for name in ("ek_p", "ev_p", "ek_s", "ev_s", "ok_p", "ov_p", "of_p", "ok_s", "ov_s", "of_s")}
    r3 = lambda a: a.reshape(bsz, seq, a.shape[-1])
    sample_rows = lambda xt: jnp.transpose(xt[0]).reshape(nseq, nh, HEAD_DIM)
    for l in range(depth):
        i = l // 2
        g_attn = attn_norm[l].reshape(1, d)
        if l % 2 == 0:
            w = w_in_even[i]
            cols = lambda j: w[:, j * half:(j + 1) * half]
            wq = jnp.concatenate([cols(0), cols(3)], axis=1).astype(BF16)
            wkv = jnp.transpose(jnp.concatenate([cols(1), cols(4), cols(2), cols(5)], axis=1)).astype(BF16)
            gq = jnp.tile(q_norm_even[i], n_mb).reshape(1, half)
            gk = jnp.tile(k_norm_even[i], n_mb).reshape(half, 1)
            kt, vt, kbt, vbt, qb, qf = _proj_even(xp, g_attn, wq, wkv, gq, gk, seg, bsz)
            o_mb = _moba_attention(r3(qb), kbt, vbt, r3(qf), kt, bias_tiles, n_mb // 2, half // LANES)
            outs["ek_p"].append(_token_major(kt, nh))
            outs["ev_p"].append(_token_major(vt, nh))
            kt_s, vt_s, _, _, _, qf_s = _proj_even(xs, g_attn, wq, wkv, gq, gk, seg, 1)
            o_sb, o_s = _attend_and_decode(
                "sb_attention_decode_even", page_table,
                functools.partial(_dec_even_kernel, n_pages=n_pages, n_sb=n_sb, chunk=MOBA_BLOCK),
                (heads_first(qf_s), kt_s.reshape(nh, HEAD_DIM, nseq), dec_bias, b0, tri_after), sample_rows(vt_s),
                (pool_view(cache_k_even[i]), pool_view(cache_v_even[i])), _page_scratch(n_pages, nh, False),
                **_sb_host(r3(qb), kbt, vbt, tri_after, n_sb // 2, SB_TILE))
            o_s = o_s.reshape(nseq, dm)
            outs["ek_s"].append(_token_major(kt_s, nh).reshape(nseq, 1, nh, HEAD_DIM))
            outs["ev_s"].append(_token_major(vt_s, nh).reshape(nseq, 1, nh, HEAD_DIM))
            w_out = w_out_even[i].astype(BF16)
            o_p = (o_sb.reshape(bsz * seq, half), o_mb.reshape(bsz * seq, half), 0, 0)
        else:
            w = w_in_odd[i]
            wq = w[:, 0:dm].astype(BF16)
            wkv = jnp.pad(jnp.transpose(w[:, dm:]), ((0, LANES - nh), (0, 0))).astype(BF16)
            b_f = jnp.pad(b_f_odd[i], (0, LANES - nh)).reshape(LANES, 1)
            gq = jnp.tile(q_norm_odd[i], nh).reshape(1, dm)
            gk = jnp.tile(k_norm_odd[i], nh).reshape(dm, 1)
            kt, vt, kbt, vbt, qb, _, lft = _proj_odd(xp, g_attn, wq, wkv, gq, gk, b_f, seg, nh, bsz)
            c_t = _cumsum_lanes(lft, tri_incl)
            outs["ok_p"].append(_token_major(kt, nh))
            outs["ov_p"].append(_token_major(vt, nh))
            outs["of_p"].append(jnp.transpose(lft, (0, 2, 1)))
            kt_s, vt_s, _, _, _, qf_s, lft_s = _proj_odd(xs, g_attn, wq, wkv, gq, gk, b_f, seg, nh, 1)
            o, o_s = _attend_and_decode(
                "fox_attention_decode_odd", page_table,
                functools.partial(_dec_odd_kernel, n_pages=n_pages, chunk=MOBA_BLOCK),
                (heads_first(qf_s), kt_s.reshape(nh, HEAD_DIM, nseq), lft_s[0], tri_after), sample_rows(vt_s),
                (pool_view(cache_k_odd[i]), pool_view(cache_v_odd[i]), jnp.transpose(cache_logf_odd[i], (0, 2, 1))),
                _page_scratch(n_pages, nh, True),
                **_fox_host(r3(qb), kbt, vbt, c_t.reshape(bsz, nh // 2, 2, seq), FOX_TILE))
            o_s = o_s.reshape(nseq, dm)
            outs["ok_s"].append(_token_major(kt_s, nh).reshape(nseq, 1, nh, HEAD_DIM))
            outs["ov_s"].append(_token_major(vt_s, nh).reshape(nseq, 1, nh, HEAD_DIM))
            outs["of_s"].append(jnp.transpose(lft_s[0]).reshape(nseq, 1, nh))
            w_out = w_out_odd[i].astype(BF16)
            o2 = o.reshape(bsz * seq, dm)
            o_p = (o2, o2, 0, 1)
        g_ffn = ffn_norm[l].reshape(1, d)
        wg, wu, wd = w_gate[l].astype(BF16), w_up[l].astype(BF16), w_down[l].astype(BF16)
        xp = _post(xp, *o_p, w_out, g_ffn, wg, wu, wd)
        xs = _post(xs, o_s, o_s, 0, 1, w_out, g_ffn, wg, wu, wd)

    st = lambda name: jnp.stack(outs[name])
    return (xp.reshape(bsz, seq, d), xs.reshape(nseq, 1, d),
            st("ek_p"), st("ev_p"), st("ek_s"), st("ev_s"),
            st("ok_p"), st("ov_p"), st("of_p"), st("ok_s"), st("ov_s"), st("of_s"))
```

```python
import functools
import math

import jax
import jax.numpy as jnp
from jax import lax
from jax.experimental import pallas as pl
from jax.experimental.pallas import tpu as pltpu

F32 = jnp.float32
BF16 = jnp.bfloat16

HEAD_DIM = 64
EPS = 1e-6
SCALE = HEAD_DIM ** -0.5
MOBA_BLOCK = 256
MOBA_TOPK = 3
PAGE_SIZE = 128
REL_BUCKETS = 32
REL_MAX_DIST = 128
NEG = -1e30
LANES = 128
MXU_DIM = 256
VMEM_LIMIT = 56 * 1024 * 1024
ROW_TILE = 512
SB_TILE = 512
MOBA_TILE = 512
FOX_TILE = 512
PV_HEAD_GROUP = 4

NT = (((1,), (1,)), ((), ()))


def _bucket_thresholds():
    max_exact = REL_BUCKETS // 2
    def bucket(d):
        return min(max_exact + int(math.log(d / max_exact) / math.log(REL_MAX_DIST / max_exact)
                                   * (REL_BUCKETS - max_exact)), REL_BUCKETS - 1)
    return [min(d for d in range(max_exact, REL_MAX_DIST + 1) if bucket(d) >= b)
            for b in range(max_exact + 1, REL_BUCKETS)]


_BUCKET_THRESHOLDS = _bucket_thresholds()


def _params(*sem):
    return pltpu.CompilerParams(dimension_semantics=sem, vmem_limit_bytes=VMEM_LIMIT)


def _const_spec(shape):
    return pl.BlockSpec(shape, lambda *_: (0,) * len(shape), pipeline_mode=pl.Buffered(1))


def _log_sigmoid_parts(z, small_relative=True):
    e = jnp.exp(-jnp.abs(z))
    if small_relative:
        sp = jnp.log1p(e)
        return jnp.minimum(z, 0.0) - sp, jnp.minimum(-z, 0.0) - sp
    lb = jnp.minimum(z, 0.0) - jnp.log(1.0 + e)
    return lb, lb - z


def _split2(x):
    hi = x.astype(BF16)
    lo = (x - hi.astype(F32)).astype(BF16)
    return hi, lo


def _split3(x):
    a = x.astype(BF16)
    r = x - a.astype(F32)
    b = r.astype(BF16)
    c = (r - b.astype(F32)).astype(BF16)
    return a, b, c


def _dot(a, b):
    return jnp.dot(a, b, preferred_element_type=F32)


def _dot_nt(a, b):
    return lax.dot_general(a, b, NT, preferred_element_type=F32)


def _rms(x, g):
    ms = jnp.mean(x * x, axis=-1, keepdims=True)
    return x * lax.rsqrt(ms + EPS) * g


def _head_rms(t, gain, seg):
    outs = []
    for c in range(t.shape[1] // MXU_DIM):
        tc = t[:, c * MXU_DIM:(c + 1) * MXU_DIM]
        hi, lo = _split2(tc * tc)
        ms = _dot(hi, seg) + _dot(lo, seg)
        outs.append(tc * lax.rsqrt(ms + EPS) * gain[:, c * MXU_DIM:(c + 1) * MXU_DIM])
    return jnp.concatenate(outs, axis=1)


def _head_rms_t(t, gain, seg):
    outs = []
    for c in range(t.shape[0] // MXU_DIM):
        tc = t[c * MXU_DIM:(c + 1) * MXU_DIM]
        hi, lo = _split2(tc * tc)
        ms = _dot(seg, hi) + _dot(seg, lo)
        outs.append(tc * lax.rsqrt(ms + EPS) * gain[c * MXU_DIM:(c + 1) * MXU_DIM])
    return jnp.concatenate(outs, axis=0)


def _proj_even_kernel(x_ref, g_ref, wq_ref, wkv_ref, gq_ref, gk_ref, seg_ref,
                      kt_ref, vt_ref, kbt_ref, vbt_ref, qb_ref, qf_ref, *, half):
    h = _rms(x_ref[...], g_ref[...]).astype(BF16)
    seg = seg_ref[...]
    uq = _dot(h, wq_ref[...])
    q = jnp.concatenate([uq[:, 0:half], _head_rms(uq[:, half:], gq_ref[...], seg)], axis=1) * SCALE
    qf_ref[...] = q
    qb_ref[...] = q.astype(BF16)
    ukv = _dot_nt(wkv_ref[...], h)
    k = jnp.concatenate([ukv[0:half], _head_rms_t(ukv[half:2 * half], gk_ref[...], seg)], axis=0)
    v = ukv[2 * half:]
    kt_ref[...] = k
    vt_ref[...] = v
    kbt_ref[...] = k.astype(BF16)
    vbt_ref[...] = v.astype(BF16)


def _feature_major_specs(groups, seq, tm, feats):
    nt = seq // tm
    spec = pl.BlockSpec((None, feats, tm), lambda i: (i // nt, 0, i % nt))
    return spec, lambda dt: jax.ShapeDtypeStruct((groups, feats, seq), dt)


def _proj_even(x, g, wq, wkv, gq, gk_col, seg, groups):
    rows, d = x.shape
    dm = wq.shape[1]
    half = dm // 2
    seq = rows // groups
    tm = min(ROW_TILE, seq)
    row = lambda w: pl.BlockSpec((tm, w), lambda i: (i, 0))
    fm, fm_shape = _feature_major_specs(groups, seq, tm, dm)
    return pl.pallas_call(
        functools.partial(_proj_even_kernel, half=half),
        grid=(rows // tm,),
        in_specs=[row(d), _const_spec((1, d)), _const_spec(wq.shape), _const_spec(wkv.shape),
                  _const_spec((1, half)), _const_spec((half, 1)), _const_spec(seg.shape)],
        out_specs=[fm, fm, fm, fm, row(dm), row(dm)],
        out_shape=[fm_shape(F32), fm_shape(F32), fm_shape(BF16), fm_shape(BF16),
                   jax.ShapeDtypeStruct((rows, dm), BF16), jax.ShapeDtypeStruct((rows, dm), F32)],
        compiler_params=_params("parallel"),
        name="proj_even",
    )(x, g, wq, wkv, gq, gk_col, seg)


def _proj_odd_kernel(x_ref, g_ref, wq_ref, wkv_ref, gq_ref, gk_ref, bf_ref, seg_ref,
                     kt_ref, vt_ref, kbt_ref, vbt_ref, qb_ref, qf_ref, lft_ref, *, dm, nh):
    h = _rms(x_ref[...], g_ref[...]).astype(BF16)
    seg = seg_ref[...]
    q = _head_rms(_dot(h, wq_ref[...]), gq_ref[...], seg) * SCALE
    qf_ref[...] = q
    qb_ref[...] = q.astype(BF16)
    ukv = _dot_nt(wkv_ref[...], h)
    k = _head_rms_t(ukv[0:dm], gk_ref[...], seg)
    v = ukv[dm:2 * dm]
    kt_ref[...] = k
    vt_ref[...] = v
    kbt_ref[...] = k.astype(BF16)
    vbt_ref[...] = v.astype(BF16)
    lf, _ = _log_sigmoid_parts(ukv[2 * dm:] + bf_ref[...])
    lft_ref[...] = lf[0:nh]


def _proj_odd(x, g, wq, wkv, gq, gk_col, bf_col, seg, nh, groups):
    rows, d = x.shape
    dm = wq.shape[1]
    seq = rows // groups
    tm = min(ROW_TILE, seq)
    row = lambda w: pl.BlockSpec((tm, w), lambda i: (i, 0))
    fm, fm_shape = _feature_major_specs(groups, seq, tm, dm)
    lfm, lfm_shape = _feature_major_specs(groups, seq, tm, nh)
    return pl.pallas_call(
        functools.partial(_proj_odd_kernel, dm=dm, nh=nh),
        grid=(rows // tm,),
        in_specs=[row(d), _const_spec((1, d)), _const_spec(wq.shape), _const_spec(wkv.shape),
                  _const_spec((1, dm)), _const_spec((dm, 1)), _const_spec((LANES, 1)), _const_spec(seg.shape)],
        out_specs=[fm, fm, fm, fm, row(dm), row(dm), lfm],
        out_shape=[fm_shape(F32), fm_shape(F32), fm_shape(BF16), fm_shape(BF16),
                   jax.ShapeDtypeStruct((rows, dm), BF16), jax.ShapeDtypeStruct((rows, dm), F32), lfm_shape(F32)],
        compiler_params=_params("parallel"),
        name="proj_odd",
    )(x, g, wq, wkv, gq, gk_col, bf_col, seg)


def _post_kernel(x_ref, oa_ref, ob_ref, wo_ref, g_ref, wg_ref, wu_ref, wd_ref, y_ref):
    ha = oa_ref.shape[1]
    x1 = (x_ref[...] + _dot(oa_ref[...].astype(BF16), wo_ref[0:ha, :])
          + _dot(ob_ref[...].astype(BF16), wo_ref[ha:, :]))
    h = _rms(x1, g_ref[...]).astype(BF16)
    gate = _dot(h, wg_ref[...])
    up = _dot(h, wu_ref[...])
    act = (gate * (1.0 / (1.0 + jnp.exp(-gate))) * up).astype(BF16)
    y_ref[...] = x1 + _dot(act, wd_ref[...])


def _post(x, oa, ob, oa_col, ob_col, wo, g, wg, wu, wd):
    rows, d = x.shape
    half = wo.shape[0] // 2
    tm = min(ROW_TILE, rows)
    return pl.pallas_call(
        _post_kernel,
        grid=(rows // tm,),
        in_specs=[pl.BlockSpec((tm, d), lambda i: (i, 0)),
                  pl.BlockSpec((tm, half), lambda i: (i, oa_col)),
                  pl.BlockSpec((tm, half), lambda i: (i, ob_col)),
                  _const_spec(wo.shape), _const_spec((1, d)), _const_spec(wg.shape),
                  _const_spec(wu.shape), _const_spec(wd.shape)],
        out_specs=pl.BlockSpec((tm, d), lambda i: (i, 0)),
        out_shape=jax.ShapeDtypeStruct((rows, d), F32),
        compiler_params=_params("parallel"),
        name="post_attn",
    )(x, oa, ob, wo, g, wg, wu, wd)


def _rel_bias_of(dist, rel_ref, h):
    d = jnp.maximum(dist, 0)
    big = jnp.full(d.shape, REL_BUCKETS // 2, jnp.int32)
    for thr in _BUCKET_THRESHOLDS:
        big = big + jnp.where(d >= thr, 1, 0)
    bucket = jnp.where(d < REL_BUCKETS // 2, d, big)
    out = jnp.full(d.shape, rel_ref[REL_BUCKETS - 1, h], F32)
    for b in range(REL_BUCKETS - 1):
        out = jnp.where(bucket == b, rel_ref[b, h], out)
    return out


def _bias_kernel(rel_ref, tiles_ref, dec_ref, *, blk, past_len):
    h = pl.program_id(0)
    row = lax.broadcasted_iota(jnp.int32, (blk, blk), 0)
    col = lax.broadcasted_iota(jnp.int32, (blk, blk), 1)
    tiles_ref[0] = _rel_bias_of(row - col, rel_ref, h)
    tiles_ref[1] = _rel_bias_of(row - col + blk, rel_ref, h)
    tiles_ref[2] = jnp.full((blk, blk), rel_ref[REL_BUCKETS - 1, h], F32)
    pos = lax.broadcasted_iota(jnp.int32, (1, past_len), 1)
    dec_ref[...] = _rel_bias_of(past_len - pos, rel_ref, h)


def _bias_tables(rel_bias, past_len, blk):
    assert blk + 1 >= REL_MAX_DIST, "tile 2 must be all in the last bucket"
    nh = rel_bias.shape[1]
    return pl.pallas_call(
        functools.partial(_bias_kernel, blk=blk, past_len=past_len),
        grid=(nh,),
        in_specs=[pl.BlockSpec(memory_space=pltpu.SMEM)],
        out_specs=[pl.BlockSpec((None, 3, blk, blk), lambda h: (h, 0, 0, 0)),
                   pl.BlockSpec((None, 1, past_len), lambda h: (h, 0, 0))],
        out_shape=[jax.ShapeDtypeStruct((nh, 3, blk, blk), F32),
                   jax.ShapeDtypeStruct((nh, 1, past_len), F32)],
        compiler_params=_params("parallel"),
        name="rel_bias_tables",
    )(rel_bias)


def _pair_queries(q2):
    lane = lax.broadcasted_iota(jnp.int32, q2.shape, 1)
    qf = q2.astype(F32)
    return [jnp.where(lane < HEAD_DIM, qf, 0.0).astype(q2.dtype), jnp.where(lane >= HEAD_DIM, qf, 0.0).astype(q2.dtype)]


def _pair_merge(a0, a1):
    lane = lax.broadcasted_iota(jnp.int32, a0.shape, 1)
    return jnp.where(lane < HEAD_DIM, a0, a1)


def _kv_tile(k_ref, v_ref, j, tk):
    start = pl.multiple_of(j * tk, tk)
    return k_ref[:, pl.ds(start, tk)], v_ref[:, pl.ds(start, tk)], start


def _sb_kernel(q_ref, k_ref, v_ref, tri_ref, o_ref, *, tq, tk):
    i = pl.program_id(2)
    qh = _pair_queries(q_ref[...])
    tri = tri_ref[...]
    row = lax.broadcasted_iota(jnp.int32, (tq, tk), 0)
    col = lax.broadcasted_iota(jnp.int32, (tq, tk), 1)

    def block(start, carry, mask):
        kt = k_ref[:, pl.ds(start, tk)]
        vt = v_ref[:, pl.ds(start, tk)]
        new = []
        for h in range(2):
            run, acc = carry[h]
            lb, l1 = _log_sigmoid_parts(_dot(qh[h], kt), small_relative=False)
            if mask is not None:
                l1 = jnp.where(mask, l1, 0.0)
            tail = _dot(l1.astype(BF16), tri)
            w = jnp.exp(lb + tail + run)
            if mask is not None:
                w = jnp.where(mask, w, 0.0)
            acc = acc + _dot_nt(w.astype(BF16), vt)
            run = run + tail[:, 0:1] + l1[:, 0:1]
            new.append((run, acc))
        return tuple(new)

    def tile(j, carry, diag):
        for blk in reversed(range(tq // tk)):
            mask = (col + blk * tk < row) if diag else None
            carry = block(pl.multiple_of(j * tq + blk * tk, tk), carry, mask)
        return carry

    init = tuple((jnp.zeros((tq, 1), F32), jnp.zeros((tq, LANES), F32)) for _ in range(2))
    carry = tile(i, init, True)
    carry = lax.fori_loop(0, i, lambda t, c: tile(i - 1 - t, c, False), carry)
    o_ref[...] = _pair_merge(carry[0][1], carry[1][1]).astype(o_ref.dtype)


def _pair_specs(t, tq, col0):
    q = pl.BlockSpec((None, tq, LANES), lambda bi, p, i, *_: (bi, i, col0 + p))
    kv = pl.BlockSpec((None, LANES, t), lambda bi, p, i, *_: (bi, col0 + p, 0))
    o = pl.BlockSpec((None, tq, LANES), lambda bi, p, i, *_: (bi, i, p))
    return q, kv, o


def _sb_host(qb, kbt, vbt, tri, n_pairs, tq):
    b, t, _ = qb.shape
    q_spec, kv_spec, o_spec = _pair_specs(t, tq, 0)
    return dict(grid=(b, n_pairs, t // tq), host=functools.partial(_sb_kernel, tq=tq, tk=tri.shape[0]),
                host_specs=[q_spec, kv_spec, kv_spec, _const_spec(tri.shape)], host_out_spec=o_spec,
                host_out_shape=jax.ShapeDtypeStruct((b, t, n_pairs * LANES), BF16), host_args=(qb, kbt, vbt, tri),
                host_scratch=[])


def _softmax_tile(s, vt, state):
    m, l, acc = state
    m_new = jnp.maximum(m, jnp.max(s, axis=1, keepdims=True))
    alpha = jnp.exp(m - m_new)
    p = jnp.exp(s - m_new)
    l = alpha * l + jnp.sum(p, axis=1, keepdims=True)
    acc = alpha * acc + _dot_nt(p.astype(BF16), vt)
    return m_new, l, acc


def _softmax_init(tq):
    return (jnp.full((tq, 1), NEG, F32), jnp.zeros((tq, 1), F32), jnp.zeros((tq, LANES), F32))


def _softmax_finish(states, o_ref):
    outs = [acc / l for (_, l, acc) in states]
    o_ref[...] = _pair_merge(outs[0], outs[1]).astype(o_ref.dtype)


def _fox_kernel(q_ref, k_ref, v_ref, ck_ref, o_ref, *, tq):
    i = pl.program_id(2)
    qh = _pair_queries(q_ref[...])
    row = lax.broadcasted_iota(jnp.int32, (tq, tq), 0)
    col = lax.broadcasted_iota(jnp.int32, (tq, tq), 1)
    causal = col <= row

    def tile(j, states, diag):
        kt, vt, start = _kv_tile(k_ref, v_ref, j, tq)
        new = []
        for h in range(2):
            s = _dot(qh[h], kt) - ck_ref[h:h + 1, pl.ds(start, tq)]
            if diag:
                s = jnp.where(causal, s, NEG)
            new.append(_softmax_tile(s, vt, states[h]))
        return tuple(new)

    states = tile(i, (_softmax_init(tq), _softmax_init(tq)), True)
    states = lax.fori_loop(0, i, lambda j, st: tile(j, st, False), states)
    _softmax_finish(states, o_ref)


def _fox_host(qb, kbt, vbt, ck, tq):
    b, t, dm = qb.shape
    q_spec, kv_spec, o_spec = _pair_specs(t, tq, 0)
    return dict(grid=(b, dm // LANES, t // tq), host=functools.partial(_fox_kernel, tq=tq),
                host_specs=[q_spec, kv_spec, kv_spec,
                            pl.BlockSpec((None, None, 2, t), lambda bi, p, i, *_: (bi, p, 0, 0))],
                host_out_spec=o_spec, host_out_shape=jax.ShapeDtypeStruct((b, t, dm), BF16),
                host_args=(qb, kbt, vbt, ck), host_scratch=[])


def _cumsum_kernel(x_ref, tri_ref, o_ref, *, chunk):
    tri = tri_ref[...]
    carry = jnp.zeros((x_ref.shape[0], 1), F32)
    for c in range(x_ref.shape[1] // chunk):
        a, b, d = _split3(x_ref[:, c * chunk:(c + 1) * chunk])
        cs = _dot(a, tri) + _dot(b, tri) + _dot(d, tri) + carry
        o_ref[:, c * chunk:(c + 1) * chunk] = cs
        carry = cs[:, chunk - 1:chunk]


def _cumsum_lanes(x, tri_incl):
    b, nh, t = x.shape
    return pl.pallas_call(
        functools.partial(_cumsum_kernel, chunk=tri_incl.shape[0]),
        grid=(b,),
        in_specs=[pl.BlockSpec((None, nh, t), lambda bi: (bi, 0, 0)), _const_spec(tri_incl.shape)],
        out_specs=pl.BlockSpec((None, nh, t), lambda bi: (bi, 0, 0)),
        out_shape=jax.ShapeDtypeStruct((b, nh, t), F32),
        compiler_params=_params("parallel"),
        name="logf_cumsum",
    )(x, tri_incl)


def _top_blocks(scores, n_valid, n_blocks, topk):
    blk = lax.broadcasted_iota(jnp.int32, scores.shape, 0)
    rank = jnp.zeros(scores.shape, F32)
    for m in range(n_blocks):
        sm = scores[m:m + 1, :]
        beats = (sm > scores) | ((sm == scores) & (m < blk))
        rank = rank + jnp.where(beats, jnp.where(m < n_valid, 1.0, 0.0), 0.0)
    return jnp.where(((rank < topk) & (blk < n_valid)) | (blk == n_valid), 1.0, 0.0)


def _moba_kernel(q_ref, k_ref, v_ref, qf_ref, kf_ref, bias_ref, o_ref, km_ref, kaug_ref, *, tq, nb):
    i = pl.program_id(2)
    t = k_ref.shape[1]
    blk_shift = MOBA_BLOCK.bit_length() - 1

    @pl.when(i == 0)
    def _():
        lane = lax.broadcasted_iota(jnp.int32, (LANES, LANES), 1)
        sums = jnp.zeros((LANES, LANES), F32)
        for n in range(nb):
            col = jnp.sum(kf_ref[:, n * MOBA_BLOCK:(n + 1) * MOBA_BLOCK], axis=1, keepdims=True)
            sums = jnp.where(lane == n, col, sums)
        means = jnp.transpose(sums)[0:nb] * (1.0 / MOBA_BLOCK)
        lane = lane[0:nb]
        km_ref[0:nb, :] = jnp.where(lane < HEAD_DIM, means, 0.0)
        km_ref[nb:, :] = jnp.where(lane >= HEAD_DIM, means, 0.0)
        kaug_ref[0:LANES, :] = k_ref[...]
        c = lax.broadcasted_iota(jnp.int32, (LANES, t), 0)
        s = lax.broadcasted_iota(jnp.int32, (LANES, t), 1)
        ind = (c < 2 * nb) & (jnp.where(c >= nb, c - nb, c) == jnp.right_shift(s, blk_shift))
        kaug_ref[LANES:, :] = jnp.where(ind, 1.0, 0.0).astype(BF16)

    ka, kb2 = _split2(km_ref[...])
    qa, qb2 = _split2(qf_ref[...])
    scores = _dot_nt(ka, qa) + _dot_nt(kb2, qa) + _dot_nt(ka, qb2)
    pos = lax.broadcasted_iota(jnp.int32, (1, tq), 1)
    own = i * (tq // MOBA_BLOCK) + jnp.right_shift(pos, blk_shift)
    sel = jnp.concatenate([_top_blocks(scores[h * nb:(h + 1) * nb], own, nb, MOBA_TOPK) for h in range(2)]
                          + [jnp.zeros((LANES - 2 * nb, tq), F32)], axis=0)
    sel_t = jnp.transpose(sel)
    colq = lax.broadcasted_iota(jnp.int32, (tq, LANES), 1)
    qh = _pair_queries(q_ref[...])
    qaug = []
    for h in range(2):
        mine = (colq >= h * nb) & (colq < (h + 1) * nb)
        neg = jnp.where(mine, jnp.where(sel_t < 0.5, NEG, 0.0), 0.0)
        qaug.append(jnp.concatenate([qh[h], neg.astype(BF16)], axis=1))

    row = lax.broadcasted_iota(jnp.int32, (tq, tq), 0)
    col = lax.broadcasted_iota(jnp.int32, (tq, tq), 1)
    causal = col <= row

    def tile(j, states, diag):
        start = pl.multiple_of(j * tq, tq)
        kt = kaug_ref[:, pl.ds(start, tq)]
        vt = v_ref[:, pl.ds(start, tq)]
        new = []
        for h in range(2):
            s = _dot(qaug[h], kt) + bias_ref[h, jnp.minimum(i - j, 2)]
            if diag:
                s = jnp.where(causal, s, NEG)
            new.append(_softmax_tile(s, vt, states[h]))
        return tuple(new)

    states = tile(i, (_softmax_init(tq), _softmax_init(tq)), True)
    states = lax.fori_loop(0, i, lambda j, st: tile(j, st, False), states)
    _softmax_finish(states, o_ref)


def _moba_attention(qb, kbt, vbt, qf, kt, bias_tiles, n_pairs, col0):
    b, t, _ = qb.shape
    tq = bias_tiles.shape[-1]
    nb = t // MOBA_BLOCK
    q_spec, kv_spec, o_spec = _pair_specs(t, tq, col0)
    return pl.pallas_call(
        functools.partial(_moba_kernel, tq=tq, nb=nb),
        grid=(b, n_pairs, t // tq),
        in_specs=[q_spec, kv_spec, kv_spec, q_spec, kv_spec,
                  pl.BlockSpec((2, 3, tq, tq), lambda bi, p, i: (p, 0, 0, 0))],
        out_specs=o_spec,
        out_shape=jax.ShapeDtypeStruct((b, t, n_pairs * LANES), BF16),
        scratch_shapes=[pltpu.VMEM((2 * nb, LANES), F32), pltpu.VMEM((2 * LANES, t), BF16)],
        compiler_params=_params("parallel", "parallel", "arbitrary"),
        name="moba_attention",
    )(qb, kbt, vbt, qf, kt, bias_tiles)


def _page_copies(pt_ref, b, slot, n_pages, pools_bufs_sems):
    cps = []
    for pool, buf, sem in pools_bufs_sems:
        for p in range(n_pages):
            cps.append(pltpu.make_async_copy(pool.at[pt_ref[b, p]], buf.at[slot, p], sem.at[slot, p]))
    return cps


def _gather_pipeline(pt_ref, b, nseq, n_pages, pools_bufs_sems):
    slot = b % 2

    @pl.when(b == 0)
    def _():
        for cp in _page_copies(pt_ref, b, slot, n_pages, pools_bufs_sems):
            cp.start()

    @pl.when(b + 1 < nseq)
    def _():
        for cp in _page_copies(pt_ref, b + 1, 1 - slot, n_pages, pools_bufs_sems):
            cp.start()

    for cp in _page_copies(pt_ref, b, slot, n_pages, pools_bufs_sems):
        cp.wait()
    return slot


def _own_column(x, b):
    lane = lax.broadcasted_iota(jnp.int32, x.shape, x.ndim - 1)
    return jnp.sum(jnp.where(lane == b, x, 0.0), axis=x.ndim - 1, keepdims=True)


def _page_logits(kbuf, slot, q_col, z_ref, n_pages, pages_per_chunk):
    nh = q_col.shape[0]
    for g in range(0, nh, PV_HEAD_GROUP):
        rows = slice(g, g + PV_HEAD_GROUP)
        q_bc = jnp.broadcast_to(q_col[rows], (PV_HEAD_GROUP, HEAD_DIM, PAGE_SIZE))
        for p in range(n_pages):
            z_ref[rows, p * PAGE_SIZE:(p + 1) * PAGE_SIZE] = jnp.sum(kbuf[slot, p, rows] * q_bc, axis=1)
    chunk = pages_per_chunk * PAGE_SIZE
    return [z_ref[:, c * chunk:(c + 1) * chunk] for c in range(n_pages // pages_per_chunk)]


def _page_values(vbuf, slot, w_chunks, n_pages, pages_per_chunk, finish, o_ref):
    nh = w_chunks[0].shape[0]
    for g in range(0, nh, PV_HEAD_GROUP):
        rows = slice(g, g + PV_HEAD_GROUP)
        acc = jnp.zeros((PV_HEAD_GROUP, HEAD_DIM, PAGE_SIZE), F32)
        for p in range(n_pages):
            c, r = divmod(p, pages_per_chunk)
            w = w_chunks[c][rows, r * PAGE_SIZE:(r + 1) * PAGE_SIZE]
            acc = acc + vbuf[slot, p, rows] * w[:, None, :]
        o_ref[rows, :] = finish(rows, jnp.sum(acc, axis=2))


def _dec_even_kernel(b, pt_ref, qt_ref, knt_ref, dbias_ref, b0_ref, tri_ref, vn_ref, kc_ref, vc_ref,
                     o_ref, z_ref, kbuf, vbuf, ksem, vsem, *, n_pages, n_sb, chunk):
    slot = _gather_pipeline(pt_ref, b, qt_ref.shape[2], n_pages, [(kc_ref, kbuf, ksem), (vc_ref, vbuf, vsem)])
    ppc = chunk // PAGE_SIZE
    nc = n_pages // ppc
    nh = qt_ref.shape[0]
    n_mb = nh - n_sb

    qt = qt_ref[...]
    zs = _page_logits(kbuf, slot, _own_column(qt, b), z_ref, n_pages, ppc)
    z_new = _own_column(jnp.sum(qt * knt_ref[...], axis=1), b)

    tri = tri_ref[...]
    run = jnp.zeros((n_sb, 1), F32)
    w_sb = [None] * nc
    for c in reversed(range(nc)):
        lb, l1 = _log_sigmoid_parts(zs[c][0:n_sb])
        hi, lo = _split2(l1)
        tail = _dot(hi, tri) + _dot(lo, tri)
        w_sb[c] = jnp.exp(lb + tail + run)
        run = run + tail[:, 0:1] + l1[:, 0:1]

    sc = [jnp.sum(zs[n][n_sb:], axis=1, keepdims=True) * (1.0 / MOBA_BLOCK) for n in range(nc)]
    lg = []
    for n in range(nc):
        rank = jnp.zeros((n_mb, 1), F32)
        for m in range(nc):
            if m != n:
                beats = (sc[m] > sc[n]) | ((sc[m] == sc[n]) & (m < n))
                rank = rank + jnp.where(beats, 1.0, 0.0)
        mask = jnp.where(rank < MOBA_TOPK, 0.0, NEG)
        lg.append(zs[n][n_sb:] + dbias_ref[:, n * chunk:(n + 1) * chunk] + mask)
    lg_new = z_new[n_sb:] + b0_ref[...]
    m = lg_new
    for n in range(nc):
        m = jnp.maximum(m, jnp.max(lg[n], axis=1, keepdims=True))
    p_new = jnp.exp(lg_new - m)
    l = p_new
    w_all = []
    for c in range(nc):
        p = jnp.exp(lg[c] - m)
        l = l + jnp.sum(p, axis=1, keepdims=True)
        w_all.append(jnp.concatenate([w_sb[c], p], axis=0))
    p_new_all = jnp.concatenate([jnp.zeros((n_sb, 1), F32), p_new], axis=0)
    inv_l = jnp.concatenate([jnp.ones((n_sb, 1), F32), 1.0 / l], axis=0)
    finish = lambda rows, acc: (acc + p_new_all[rows] * vn_ref[rows, :]) * inv_l[rows]
    _page_values(vbuf, slot, w_all, n_pages, ppc, finish, o_ref)


def _dec_odd_kernel(b, pt_ref, qt_ref, knt_ref, lfn_ref, tri_ref, vn_ref, kc_ref, vc_ref, lc_ref,
                    o_ref, z_ref, kbuf, vbuf, lbuf, ksem, vsem, lsem, *, n_pages, chunk):
    slot = _gather_pipeline(pt_ref, b, qt_ref.shape[2], n_pages,
                            [(kc_ref, kbuf, ksem), (vc_ref, vbuf, vsem), (lc_ref, lbuf, lsem)])
    ppc = chunk // PAGE_SIZE
    nc = n_pages // ppc

    qt = qt_ref[...]
    zs = _page_logits(kbuf, slot, _own_column(qt, b), z_ref, n_pages, ppc)
    lg_new = _own_column(jnp.sum(qt * knt_ref[...], axis=1), b)
    tri = tri_ref[...]
    run = _own_column(lfn_ref[...], b)
    lg = [None] * nc
    for c in reversed(range(nc)):
        lf_c = jnp.concatenate([lbuf[slot, p] for p in range(c * ppc, (c + 1) * ppc)], axis=1)
        a, b3, d = _split3(lf_c)
        tail = _dot(a, tri) + _dot(b3, tri) + _dot(d, tri)
        lg[c] = zs[c] + tail + run
        run = run + tail[:, 0:1] + lf_c[:, 0:1]
    m = lg_new
    for c in range(nc):
        m = jnp.maximum(m, jnp.max(lg[c], axis=1, keepdims=True))
    p_new = jnp.exp(lg_new - m)
    l = p_new
    w_all = []
    for c in range(nc):
        p = jnp.exp(lg[c] - m)
        l = l + jnp.sum(p, axis=1, keepdims=True)
        w_all.append(p)
    finish = lambda rows, acc: (acc + p_new[rows] * vn_ref[rows, :]) / l[rows]
    _page_values(vbuf, slot, w_all, n_pages, ppc, finish, o_ref)


def _page_scratch(n_pages, nh, with_logf):
    page = (2, n_pages, nh, HEAD_DIM, PAGE_SIZE)
    bufs = [pltpu.VMEM(page, F32), pltpu.VMEM(page, F32)]
    if with_logf:
        bufs.append(pltpu.VMEM((2, n_pages, nh, PAGE_SIZE), F32))
    logits = pltpu.VMEM((nh, n_pages * PAGE_SIZE), F32)
    return [logits] + bufs + [pltpu.SemaphoreType.DMA((2, n_pages)) for _ in bufs]


def _attend_and_decode_kernel(pt_ref, *refs, host, n_host_in, n_host_scratch, dec, n_dec_in, steps_per_seq):
    host_in = refs[:n_host_in]
    dec_in = refs[n_host_in:n_host_in + n_dec_in]
    host_out, dec_out = refs[n_host_in + n_dec_in:n_host_in + n_dec_in + 2]
    scratch = refs[n_host_in + n_dec_in + 2:]
    host_scratch, dec_scratch = scratch[:n_host_scratch], scratch[n_host_scratch:]
    step = ((pl.program_id(0) * pl.num_programs(1) + pl.program_id(1)) * pl.num_programs(2)
            + pl.program_id(2))

    @pl.when(step % steps_per_seq == 0)
    def _():
        dec(step // steps_per_seq, pt_ref, *dec_in, dec_out, *dec_scratch)

    host(*host_in, host_out, *host_scratch)


def _attend_and_decode(name, page_table, dec, dec_consts, vn, pools, scratch, *, grid, host, host_specs,
                       host_out_spec, host_out_shape, host_args, host_scratch):
    nseq = page_table.shape[0]
    nh = vn.shape[1]
    steps = math.prod(grid)
    assert steps % nseq == 0
    spp = steps // nseq
    per_seq = pl.BlockSpec((None, nh, HEAD_DIM),
                           lambda bi, p, i, pt: (((bi * grid[1] + p) * grid[2] + i) // spp, 0, 0))
    dec_specs = ([_const_spec(a.shape) for a in dec_consts] + [per_seq]
                 + [pl.BlockSpec(memory_space=pl.ANY) for _ in pools])
    grid_spec = pltpu.PrefetchScalarGridSpec(
        num_scalar_prefetch=1, grid=grid, in_specs=list(host_specs) + dec_specs,
        out_specs=[host_out_spec, per_seq], scratch_shapes=list(host_scratch) + list(scratch))
    return pl.pallas_call(
        functools.partial(_attend_and_decode_kernel, host=host, n_host_in=len(host_args),
                          n_host_scratch=len(host_scratch), dec=dec,
                          n_dec_in=len(dec_specs), steps_per_seq=spp),
        grid_spec=grid_spec,
        out_shape=[host_out_shape, jax.ShapeDtypeStruct((nseq, nh, HEAD_DIM), F32)],
        compiler_params=_params("arbitrary", "arbitrary", "arbitrary"),
        name=name,
    )(page_table, *host_args, *dec_consts, vn, *pools)


def _tri(n, keep):
    r = lax.broadcasted_iota(jnp.int32, (n, n), 0)
    c = lax.broadcasted_iota(jnp.int32, (n, n), 1)
    return keep(r, c).astype(BF16)


def _token_major(xt, nh):
    g, _, seq = xt.shape
    return jnp.transpose(xt.reshape(g, nh, HEAD_DIM, seq), (0, 3, 1, 2))


def kernel(x_prompt, x_sample, cache_k_even, cache_v_even, cache_k_odd, cache_v_odd, cache_logf_odd, page_table, attn_norm, ffn_norm, w_in_even, w_out_even, q_norm_even, k_norm_even, rel_bias, w_in_odd, b_f_odd, w_out_odd, q_norm_odd, k_norm_odd, w_gate, w_up, w_down):
    bsz, seq, d = x_prompt.shape
    nseq, dec_seq, _ = x_sample.shape
    depth = attn_norm.shape[0]
    page = cache_k_even.shape[2]
    nh = cache_k_even.shape[3]
    dm = nh * HEAD_DIM
    n_mb = rel_bias.shape[1]
    n_sb = nh - n_mb
    half = n_sb * HEAD_DIM
    n_pages = page_table.shape[1]
    past = n_pages * page
    assert dec_seq == 1 and page == PAGE_SIZE and n_sb == n_mb and n_sb % PV_HEAD_GROUP == 0
    assert seq % MOBA_TILE == 0 and past % MOBA_BLOCK == 0 and half % MXU_DIM == 0 and MOBA_TILE % MOBA_BLOCK == 0

    tri_after = _tri(MOBA_BLOCK, lambda r, c: r > c)
    tri_incl = _tri(MOBA_BLOCK, lambda r, c: r <= c)
    seg = _tri(MXU_DIM, lambda r, c: (r // HEAD_DIM) == (c // HEAD_DIM)) * (1.0 / HEAD_DIM)

    pool_view = lambda cache: jnp.transpose(cache, (0, 2, 3, 1))
    heads_first = lambda a: jnp.transpose(a.reshape(nseq, nh, HEAD_DIM), (1, 2, 0))

    xp = x_prompt.reshape(bsz * seq, d)
    xs = x_sample.reshape(nseq, d)
    bias_tiles, dec_bias = _bias_tables(rel_bias, past, MOBA_TILE)
    dec_bias = dec_bias.reshape(n_mb, past)
    b0 = rel_bias[0].reshape(n_mb, 1)

    outs = {name: [] for name in ("ek_p", "ev_p", "ek_s", "ev_s", "ok_p", "ov_p", "of_p", "ok_s", "ov_s", "of_s")}
    r3 = lambda a: a.reshape(bsz, seq, a.shape[-1])
    sample_rows = lambda xt: jnp.transpose(xt[0]).reshape(nseq, nh, HEAD_DIM)
    for l in range(depth):
        i = l // 2
        g_attn = attn_norm[l].reshape(1, d)
        if l % 2 == 0:
            w = w_in_even[i]
            cols = lambda j: w[:, j * half:(j + 1) * half]
            wq = jnp.concatenate([cols(0), cols(3)], axis=1).astype(BF16)
            wkv = jnp.transpose(jnp.concatenate([cols(1), cols(4), cols(2), cols(5)], axis=1)).astype(BF16)
            gq = jnp.tile(q_norm_even[i], n_mb).reshape(1, half)
            gk = jnp.tile(k_norm_even[i], n_mb).reshape(half, 1)
            kt, vt, kbt, vbt, qb, qf = _proj_even(xp, g_attn, wq, wkv, gq, gk, seg, bsz)
            o_mb = _moba_attention(r3(qb), kbt, vbt, r3(qf), kt, bias_tiles, n_mb // 2, half // LANES)
            outs["ek_p"].append(_token_major(kt, nh))
            outs["ev_p"].append(_token_major(vt, nh))
            kt_s, vt_s, _, _, _, qf_s = _proj_even(xs, g_attn, wq, wkv, gq, gk, seg, 1)
            o_sb, o_s = _attend_and_decode(
                "sb_attention_decode_even", page_table,
                functools.partial(_dec_even_kernel, n_pages=n_pages, n_sb=n_sb, chunk=MOBA_BLOCK),
                (heads_first(qf_s), kt_s.reshape(nh, HEAD_DIM, nseq), dec_bias, b0, tri_after), sample_rows(vt_s),
                (pool_view(cache_k_even[i]), pool_view(cache_v_even[i])), _page_scratch(n_pages, nh, False),
                **_sb_host(r3(qb), kbt, vbt, tri_after, n_sb // 2, SB_TILE))
            o_s = o_s.reshape(nseq, dm)
            outs["ek_s"].append(_token_major(kt_s, nh).reshape(nseq, 1, nh, HEAD_DIM))
            outs["ev_s"].append(_token_major(vt_s, nh).reshape(nseq, 1, nh, HEAD_DIM))
            w_out = w_out_even[i].astype(BF16)
            o_p = (o_sb.reshape(bsz * seq, half), o_mb.reshape(bsz * seq, half), 0, 0)
        else:
            w = w_in_odd[i]
            wq = w[:, 0:dm].astype(BF16)
            wkv = jnp.pad(jnp.transpose(w[:, dm:]), ((0, LANES - nh), (0, 0))).astype(BF16)
            b_f = jnp.pad(b_f_odd[i], (0, LANES - nh)).reshape(LANES, 1)
            gq = jnp.tile(q_norm_odd[i], nh).reshape(1, dm)
            gk = jnp.tile(k_norm_odd[i], nh).reshape(dm, 1)
            kt, vt, kbt, vbt, qb, _, lft = _proj_odd(xp, g_attn, wq, wkv, gq, gk, b_f, seg, nh, bsz)
            c_t = _cumsum_lanes(lft, tri_incl)
            outs["ok_p"].append(_token_major(kt, nh))
            outs["ov_p"].append(_token_major(vt, nh))
            outs["of_p"].append(jnp.transpose(lft, (0, 2, 1)))
            kt_s, vt_s, _, _, _, qf_s, lft_s = _proj_odd(xs, g_attn, wq, wkv, gq, gk, b_f, seg, nh, 1)
            o, o_s = _attend_and_decode(
                "fox_attention_decode_odd", page_table,
                functools.partial(_dec_odd_kernel, n_pages=n_pages, chunk=MOBA_BLOCK),
                (heads_first(qf_s), kt_s.reshape(nh, HEAD_DIM, nseq), lft_s[0], tri_after), sample_rows(vt_s),
                (pool_view(cache_k_odd[i]), pool_view(cache_v_odd[i]), jnp.transpose(cache_logf_odd[i], (0, 2, 1))),
                _page_scratch(n_pages, nh, True),
                **_fox_host(r3(qb), kbt, vbt, c_t.reshape(bsz, nh // 2, 2, seq), FOX_TILE))
            o_s = o_s.reshape(nseq, dm)
            outs["ok_s"].append(_token_major(kt_s, nh).reshape(nseq, 1, nh, HEAD_DIM))
            outs["ov_s"].append(_token_major(vt_s, nh).reshape(nseq, 1, nh, HEAD_DIM))
            outs["of_s"].append(jnp.transpose(lft_s[0]).reshape(nseq, 1, nh))
            w_out = w_out_odd[i].astype(BF16)
            o2 = o.reshape(bsz * seq, dm)
            o_p = (o2, o2, 0, 1)
        g_ffn = ffn_norm[l].reshape(1, d)
        wg, wu, wd = w_gate[l].astype(BF16), w_up[l].astype(BF16), w_down[l].astype(BF16)
        xp = _post(xp, *o_p, w_out, g_ffn, wg, wu, wd)
        xs = _post(xs, o_s, o_s, 0, 1, w_out, g_ffn, wg, wu, wd)

    st = lambda name: jnp.stack(outs[name])
    return (xp.reshape(bsz, seq, d), xs.reshape(nseq, 1, d),
            st("ek_p"), st("ev_p"), st("ek_s"), st("ev_s"),
            st("ok_p"), st("ov_p"), st("of_p"), st("ok_s"), st("ov_s"), st("of_s"))
```

```python
import functools
import math

import jax
import jax.numpy as jnp
from jax import lax
from jax.experimental import pallas as pl
from jax.experimental.pallas import tpu as pltpu

F32 = jnp.float32
BF16 = jnp.bfloat16

HEAD_DIM = 64
EPS = 1e-6
SCALE = HEAD_DIM ** -0.5
MOBA_BLOCK = 256
MOBA_TOPK = 3
PAGE_SIZE = 128
REL_BUCKETS = 32
REL_MAX_DIST = 128
NEG = -1e30
LANES = 128
MXU_DIM = 256
VMEM_LIMIT = 56 * 1024 * 1024
ROW_TILE = 512
SB_TILE = 512
MOBA_TILE = 512
FOX_TILE = 1024
PV_HEAD_GROUP = 4

NT = (((1,), (1,)), ((), ()))


def _bucket_thresholds():
    max_exact = REL_BUCKETS // 2
    def bucket(d):
        return min(max_exact + int(math.log(d / max_exact) / math.log(REL_MAX_DIST / max_exact)
                                   * (REL_BUCKETS - max_exact)), REL_BUCKETS - 1)
    return [min(d for d in range(max_exact, REL_MAX_DIST + 1) if bucket(d) >= b)
            for b in range(max_exact + 1, REL_BUCKETS)]


_BUCKET_THRESHOLDS = _bucket_thresholds()


def _params(*sem):
    return pltpu.CompilerParams(dimension_semantics=sem, vmem_limit_bytes=VMEM_LIMIT)


def _const_spec(shape):
    return pl.BlockSpec(shape, lambda *_: (0,) * len(shape), pipeline_mode=pl.Buffered(1))


def _log_sigmoid_parts(z, small_relative=True):
    e = jnp.exp(-jnp.abs(z))
    if small_relative:
        sp = jnp.log1p(e)
        return jnp.minimum(z, 0.0) - sp, jnp.minimum(-z, 0.0) - sp
    lb = jnp.minimum(z, 0.0) - jnp.log(1.0 + e)
    return lb, lb - z


def _split2(x):
    hi = x.astype(BF16)
    lo = (x - hi.astype(F32)).astype(BF16)
    return hi, lo


def _split3(x):
    a = x.astype(BF16)
    r = x - a.astype(F32)
    b = r.astype(BF16)
    c = (r - b.astype(F32)).astype(BF16)
    return a, b, c


def _dot(a, b):
    return jnp.dot(a, b, preferred_element_type=F32)


def _dot_nt(a, b):
    return lax.dot_general(a, b, NT, preferred_element_type=F32)


def _rms(x, g):
    ms = jnp.mean(x * x, axis=-1, keepdims=True)
    return x * lax.rsqrt(ms + EPS) * g


def _head_rms(t, gain, seg):
    outs = []
    for c in range(t.shape[1] // MXU_DIM):
        tc = t[:, c * MXU_DIM:(c + 1) * MXU_DIM]
        hi, lo = _split2(tc * tc)
        ms = _dot(hi, seg) + _dot(lo, seg)
        outs.append(tc * lax.rsqrt(ms + EPS) * gain[:, c * MXU_DIM:(c + 1) * MXU_DIM])
    return jnp.concatenate(outs, axis=1)


def _head_rms_t(t, gain, seg):
    outs = []
    for c in range(t.shape[0] // MXU_DIM):
        tc = t[c * MXU_DIM:(c + 1) * MXU_DIM]
        hi, lo = _split2(tc * tc)
        ms = _dot(seg, hi) + _dot(seg, lo)
        outs.append(tc * lax.rsqrt(ms + EPS) * gain[c * MXU_DIM:(c + 1) * MXU_DIM])
    return jnp.concatenate(outs, axis=0)


def _proj_even_kernel(x_ref, g_ref, wq_ref, wkv_ref, gq_ref, gk_ref, seg_ref,
                      kt_ref, vt_ref, kbt_ref, vbt_ref, qb_ref, qf_ref, *, half):
    h = _rms(x_ref[...], g_ref[...]).astype(BF16)
    seg = seg_ref[...]
    uq = _dot(h, wq_ref[...])
    q = jnp.concatenate([uq[:, 0:half], _head_rms(uq[:, half:], gq_ref[...], seg)], axis=1) * SCALE
    qf_ref[...] = q
    qb_ref[...] = q.astype(BF16)
    ukv = _dot_nt(wkv_ref[...], h)
    k = jnp.concatenate([ukv[0:half], _head_rms_t(ukv[half:2 * half], gk_ref[...], seg)], axis=0)
    v = ukv[2 * half:]
    kt_ref[...] = k
    vt_ref[...] = v
    kbt_ref[...] = k.astype(BF16)
    vbt_ref[...] = v.astype(BF16)


def _feature_major_specs(groups, seq, tm, feats):
    nt = seq // tm
    spec = pl.BlockSpec((None, feats, tm), lambda i: (i // nt, 0, i % nt))
    return spec, lambda dt: jax.ShapeDtypeStruct((groups, feats, seq), dt)


def _proj_even(x, g, wq, wkv, gq, gk_col, seg, groups):
    rows, d = x.shape
    dm = wq.shape[1]
    half = dm // 2
    seq = rows // groups
    tm = min(ROW_TILE, seq)
    row = lambda w: pl.BlockSpec((tm, w), lambda i: (i, 0))
    fm, fm_shape = _feature_major_specs(groups, seq, tm, dm)
    return pl.pallas_call(
        functools.partial(_proj_even_kernel, half=half),
        grid=(rows // tm,),
        in_specs=[row(d), _const_spec((1, d)), _const_spec(wq.shape), _const_spec(wkv.shape),
                  _const_spec((1, half)), _const_spec((half, 1)), _const_spec(seg.shape)],
        out_specs=[fm, fm, fm, fm, row(dm), row(dm)],
        out_shape=[fm_shape(F32), fm_shape(F32), fm_shape(BF16), fm_shape(BF16),
                   jax.ShapeDtypeStruct((rows, dm), BF16), jax.ShapeDtypeStruct((rows, dm), F32)],
        compiler_params=_params("parallel"),
        name="proj_even",
    )(x, g, wq, wkv, gq, gk_col, seg)


def _proj_odd_kernel(x_ref, g_ref, wq_ref, wkv_ref, gq_ref, gk_ref, bf_ref, seg_ref,
                     kt_ref, vt_ref, kbt_ref, vbt_ref, qb_ref, qf_ref, lft_ref, *, dm, nh):
    h = _rms(x_ref[...], g_ref[...]).astype(BF16)
    seg = seg_ref[...]
    q = _head_rms(_dot(h, wq_ref[...]), gq_ref[...], seg) * SCALE
    qf_ref[...] = q
    qb_ref[...] = q.astype(BF16)
    ukv = _dot_nt(wkv_ref[...], h)
    k = _head_rms_t(ukv[0:dm], gk_ref[...], seg)
    v = ukv[dm:2 * dm]
    kt_ref[...] = k
    vt_ref[...] = v
    kbt_ref[...] = k.astype(BF16)
    vbt_ref[...] = v.astype(BF16)
    lf, _ = _log_sigmoid_parts(ukv[2 * dm:] + bf_ref[...])
    lft_ref[...] = lf[0:nh]


def _proj_odd(x, g, wq, wkv, gq, gk_col, bf_col, seg, nh, groups):
    rows, d = x.shape
    dm = wq.shape[1]
    seq = rows // groups
    tm = min(ROW_TILE, seq)
    row = lambda w: pl.BlockSpec((tm, w), lambda i: (i, 0))
    fm, fm_shape = _feature_major_specs(groups, seq, tm, dm)
    lfm, lfm_shape = _feature_major_specs(groups, seq, tm, nh)
    return pl.pallas_call(
        functools.partial(_proj_odd_kernel, dm=dm, nh=nh),
        grid=(rows // tm,),
        in_specs=[row(d), _const_spec((1, d)), _const_spec(wq.shape), _const_spec(wkv.shape),
                  _const_spec((1, dm)), _const_spec((dm, 1)), _const_spec((LANES, 1)), _const_spec(seg.shape)],
        out_specs=[fm, fm, fm, fm, row(dm), row(dm), lfm],
        out_shape=[fm_shape(F32), fm_shape(F32), fm_shape(BF16), fm_shape(BF16),
                   jax.ShapeDtypeStruct((rows, dm), BF16), jax.ShapeDtypeStruct((rows, dm), F32), lfm_shape(F32)],
        compiler_params=_params("parallel"),
        name="proj_odd",
    )(x, g, wq, wkv, gq, gk_col, bf_col, seg)


def _post_kernel(x_ref, oa_ref, ob_ref, wo_ref, g_ref, wg_ref, wu_ref, wd_ref, y_ref):
    ha = oa_ref.shape[1]
    x1 = (x_ref[...] + _dot(oa_ref[...].astype(BF16), wo_ref[0:ha, :])
          + _dot(ob_ref[...].astype(BF16), wo_ref[ha:, :]))
    h = _rms(x1, g_ref[...]).astype(BF16)
    gate = _dot(h, wg_ref[...])
    up = _dot(h, wu_ref[...])
    act = (gate * (1.0 / (1.0 + jnp.exp(-gate))) * up).astype(BF16)
    y_ref[...] = x1 + _dot(act, wd_ref[...])


def _post(x, oa, ob, oa_col, ob_col, wo, g, wg, wu, wd):
    rows, d = x.shape
    half = wo.shape[0] // 2
    tm = min(ROW_TILE, rows)
    return pl.pallas_call(
        _post_kernel,
        grid=(rows // tm,),
        in_specs=[pl.BlockSpec((tm, d), lambda i: (i, 0)),
                  pl.BlockSpec((tm, half), lambda i: (i, oa_col)),
                  pl.BlockSpec((tm, half), lambda i: (i, ob_col)),
                  _const_spec(wo.shape), _const_spec((1, d)), _const_spec(wg.shape),
                  _const_spec(wu.shape), _const_spec(wd.shape)],
        out_specs=pl.BlockSpec((tm, d), lambda i: (i, 0)),
        out_shape=jax.ShapeDtypeStruct((rows, d), F32),
        compiler_params=_params("parallel"),
        name="post_attn",
    )(x, oa, ob, wo, g, wg, wu, wd)


def _rel_bias_of(dist, rel_ref, h):
    d = jnp.maximum(dist, 0)
    big = jnp.full(d.shape, REL_BUCKETS // 2, jnp.int32)
    for thr in _BUCKET_THRESHOLDS:
        big = big + jnp.where(d >= thr, 1, 0)
    bucket = jnp.where(d < REL_BUCKETS // 2, d, big)
    out = jnp.full(d.shape, rel_ref[REL_BUCKETS - 1, h], F32)
    for b in range(REL_BUCKETS - 1):
        out = jnp.where(bucket == b, rel_ref[b, h], out)
    return out


def _bias_kernel(rel_ref, tiles_ref, dec_ref, *, blk, past_len):
    h = pl.program_id(0)
    row = lax.broadcasted_iota(jnp.int32, (blk, blk), 0)
    col = lax.broadcasted_iota(jnp.int32, (blk, blk), 1)
    tiles_ref[0] = _rel_bias_of(row - col, rel_ref, h)
    tiles_ref[1] = _rel_bias_of(row - col + blk, rel_ref, h)
    tiles_ref[2] = jnp.full((blk, blk), rel_ref[REL_BUCKETS - 1, h], F32)
    pos = lax.broadcasted_iota(jnp.int32, (1, past_len), 1)
    dec_ref[...] = _rel_bias_of(past_len - pos, rel_ref, h)


def _bias_tables(rel_bias, past_len, blk):
    assert blk + 1 >= REL_MAX_DIST, "tile 2 must be all in the last bucket"
    nh = rel_bias.shape[1]
    return pl.pallas_call(
        functools.partial(_bias_kernel, blk=blk, past_len=past_len),
        grid=(nh,),
        in_specs=[pl.BlockSpec(memory_space=pltpu.SMEM)],
        out_specs=[pl.BlockSpec((None, 3, blk, blk), lambda h: (h, 0, 0, 0)),
                   pl.BlockSpec((None, 1, past_len), lambda h: (h, 0, 0))],
        out_shape=[jax.ShapeDtypeStruct((nh, 3, blk, blk), F32),
                   jax.ShapeDtypeStruct((nh, 1, past_len), F32)],
        compiler_params=_params("parallel"),
        name="rel_bias_tables",
    )(rel_bias)


def _pair_queries(q2):
    lane = lax.broadcasted_iota(jnp.int32, q2.shape, 1)
    qf = q2.astype(F32)
    return [jnp.where(lane < HEAD_DIM, qf, 0.0).astype(q2.dtype), jnp.where(lane >= HEAD_DIM, qf, 0.0).astype(q2.dtype)]


def _pair_merge(a0, a1):
    lane = lax.broadcasted_iota(jnp.int32, a0.shape, 1)
    return jnp.where(lane < HEAD_DIM, a0, a1)


def _kv_tile(k_ref, v_ref, j, tk):
    start = pl.multiple_of(j * tk, tk)
    return k_ref[:, pl.ds(start, tk)], v_ref[:, pl.ds(start, tk)], start


def _sb_kernel(q_ref, k_ref, v_ref, tri_ref, o_ref, *, tq, tk):
    i = pl.program_id(2)
    qh = _pair_queries(q_ref[...])
    tri = tri_ref[...]
    row = lax.broadcasted_iota(jnp.int32, (tq, tk), 0)
    col = lax.broadcasted_iota(jnp.int32, (tq, tk), 1)

    def block(start, carry, mask):
        kt = k_ref[:, pl.ds(start, tk)]
        vt = v_ref[:, pl.ds(start, tk)]
        new = []
        for h in range(2):
            run, acc = carry[h]
            lb, l1 = _log_sigmoid_parts(_dot(qh[h], kt), small_relative=False)
            if mask is not None:
                l1 = jnp.where(mask, l1, 0.0)
            tail = _dot(l1.astype(BF16), tri)
            w = jnp.exp(lb + tail + run)
            if mask is not None:
                w = jnp.where(mask, w, 0.0)
            acc = acc + _dot_nt(w.astype(BF16), vt)
            run = run + tail[:, 0:1] + l1[:, 0:1]
            new.append((run, acc))
        return tuple(new)

    def tile(j, carry, diag):
        for blk in reversed(range(tq // tk)):
            mask = (col + blk * tk < row) if diag else None
            carry = block(pl.multiple_of(j * tq + blk * tk, tk), carry, mask)
        return carry

    init = tuple((jnp.zeros((tq, 1), F32), jnp.zeros((tq, LANES), F32)) for _ in range(2))
    carry = tile(i, init, True)
    carry = lax.fori_loop(0, i, lambda t, c: tile(i - 1 - t, c, False), carry)
    o_ref[...] = _pair_merge(carry[0][1], carry[1][1]).astype(o_ref.dtype)


def _pair_specs(t, tq, col0):
    q = pl.BlockSpec((None, tq, LANES), lambda bi, p, i, *_: (bi, i, col0 + p))
    kv = pl.BlockSpec((None, LANES, t), lambda bi, p, i, *_: (bi, col0 + p, 0))
    o = pl.BlockSpec((None, tq, LANES), lambda bi, p, i, *_: (bi, i, p))
    return q, kv, o


def _sb_host(qb, kbt, vbt, tri, n_pairs, tq):
    b, t, _ = qb.shape
    q_spec, kv_spec, o_spec = _pair_specs(t, tq, 0)
    return dict(grid=(b, n_pairs, t // tq), host=functools.partial(_sb_kernel, tq=tq, tk=tri.shape[0]),
                host_specs=[q_spec, kv_spec, kv_spec, _const_spec(tri.shape)], host_out_spec=o_spec,
                host_out_shape=jax.ShapeDtypeStruct((b, t, n_pairs * LANES), BF16), host_args=(qb, kbt, vbt, tri),
                host_scratch=[])


def _softmax_tile(s, vt, state):
    m, l, acc = state
    m_new = jnp.maximum(m, jnp.max(s, axis=1, keepdims=True))
    alpha = jnp.exp(m - m_new)
    p = jnp.exp(s - m_new)
    l = alpha * l + jnp.sum(p, axis=1, keepdims=True)
    acc = alpha * acc + _dot_nt(p.astype(BF16), vt)
    return m_new, l, acc


def _softmax_init(tq):
    return (jnp.full((tq, 1), NEG, F32), jnp.zeros((tq, 1), F32), jnp.zeros((tq, LANES), F32))


def _softmax_finish(states, o_ref):
    outs = [acc / l for (_, l, acc) in states]
    o_ref[...] = _pair_merge(outs[0], outs[1]).astype(o_ref.dtype)


def _fox_kernel(q_ref, k_ref, v_ref, ck_ref, o_ref, *, tq):
    i = pl.program_id(2)
    qh = _pair_queries(q_ref[...])
    row = lax.broadcasted_iota(jnp.int32, (tq, tq), 0)
    col = lax.broadcasted_iota(jnp.int32, (tq, tq), 1)
    causal = col <= row

    def tile(j, states, diag):
        kt, vt, start = _kv_tile(k_ref, v_ref, j, tq)
        new = []
        for h in range(2):
            s = _dot(qh[h], kt) - ck_ref[h:h + 1, pl.ds(start, tq)]
            if diag:
                s = jnp.where(causal, s, NEG)
            new.append(_softmax_tile(s, vt, states[h]))
        return tuple(new)

    states = tile(i, (_softmax_init(tq), _softmax_init(tq)), True)
    states = lax.fori_loop(0, i, lambda j, st: tile(j, st, False), states)
    _softmax_finish(states, o_ref)


def _fox_host(qb, kbt, vbt, ck, tq):
    b, t, dm = qb.shape
    q_spec, kv_spec, o_spec = _pair_specs(t, tq, 0)
    return dict(grid=(b, dm // LANES, t // tq), host=functools.partial(_fox_kernel, tq=tq),
                host_specs=[q_spec, kv_spec, kv_spec,
                            pl.BlockSpec((None, None, 2, t), lambda bi, p, i, *_: (bi, p, 0, 0))],
                host_out_spec=o_spec, host_out_shape=jax.ShapeDtypeStruct((b, t, dm), BF16),
                host_args=(qb, kbt, vbt, ck), host_scratch=[])


def _cumsum_kernel(x_ref, tri_ref, o_ref, *, chunk):
    tri = tri_ref[...]
    carry = jnp.zeros((x_ref.shape[0], 1), F32)
    for c in range(x_ref.shape[1] // chunk):
        a, b, d = _split3(x_ref[:, c * chunk:(c + 1) * chunk])
        cs = _dot(a, tri) + _dot(b, tri) + _dot(d, tri) + carry
        o_ref[:, c * chunk:(c + 1) * chunk] = cs
        carry = cs[:, chunk - 1:chunk]


def _cumsum_lanes(x, tri_incl):
    b, nh, t = x.shape
    return pl.pallas_call(
        functools.partial(_cumsum_kernel, chunk=tri_incl.shape[0]),
        grid=(b,),
        in_specs=[pl.BlockSpec((None, nh, t), lambda bi: (bi, 0, 0)), _const_spec(tri_incl.shape)],
        out_specs=pl.BlockSpec((None, nh, t), lambda bi: (bi, 0, 0)),
        out_shape=jax.ShapeDtypeStruct((b, nh, t), F32),
        compiler_params=_params("parallel"),
        name="logf_cumsum",
    )(x, tri_incl)


def _top_blocks(scores, n_valid, n_blocks, topk):
    blk = lax.broadcasted_iota(jnp.int32, scores.shape, 0)
    rank = jnp.zeros(scores.shape, F32)
    for m in range(n_blocks):
        sm = scores[m:m + 1, :]
        beats = (sm > scores) | ((sm == scores) & (m < blk))
        rank = rank + jnp.where(beats, jnp.where(m < n_valid, 1.0, 0.0), 0.0)
    return jnp.where(((rank < topk) & (blk < n_valid)) | (blk == n_valid), 1.0, 0.0)


def _moba_kernel(q_ref, k_ref, v_ref, qf_ref, kf_ref, bias_ref, o_ref, km_ref, kaug_ref, *, tq, nb):
    i = pl.program_id(2)
    t = k_ref.shape[1]
    blk_shift = MOBA_BLOCK.bit_length() - 1

    @pl.when(i == 0)
    def _():
        lane = lax.broadcasted_iota(jnp.int32, (LANES, LANES), 1)
        sums = jnp.zeros((LANES, LANES), F32)
        for n in range(nb):
            col = jnp.sum(kf_ref[:, n * MOBA_BLOCK:(n + 1) * MOBA_BLOCK], axis=1, keepdims=True)
            sums = jnp.where(lane == n, col, sums)
        means = jnp.transpose(sums)[0:nb] * (1.0 / MOBA_BLOCK)
        lane = lane[0:nb]
        km_ref[0:nb, :] = jnp.where(lane < HEAD_DIM, means, 0.0)
        km_ref[nb:, :] = jnp.where(lane >= HEAD_DIM, means, 0.0)
        kaug_ref[0:LANES, :] = k_ref[...]
        c = lax.broadcasted_iota(jnp.int32, (LANES, t), 0)
        s = lax.broadcasted_iota(jnp.int32, (LANES, t), 1)
        ind = (c < 2 * nb) & (jnp.where(c >= nb, c - nb, c) == jnp.right_shift(s, blk_shift))
        kaug_ref[LANES:, :] = jnp.where(ind, 1.0, 0.0).astype(BF16)

    ka, kb2 = _split2(km_ref[...])
    qa, qb2 = _split2(qf_ref[...])
    scores = _dot_nt(ka, qa) + _dot_nt(kb2, qa) + _dot_nt(ka, qb2)
    pos = lax.broadcasted_iota(jnp.int32, (1, tq), 1)
    own = i * (tq // MOBA_BLOCK) + jnp.right_shift(pos, blk_shift)
    sel = jnp.concatenate([_top_blocks(scores[h * nb:(h + 1) * nb], own, nb, MOBA_TOPK) for h in range(2)]
                          + [jnp.zeros((LANES - 2 * nb, tq), F32)], axis=0)
    sel_t = jnp.transpose(sel)
    colq = lax.broadcasted_iota(jnp.int32, (tq, LANES), 1)
    qh = _pair_queries(q_ref[...])
    qaug = []
    for h in range(2):
        mine = (colq >= h * nb) & (colq < (h + 1) * nb)
        neg = jnp.where(mine, jnp.where(sel_t < 0.5, NEG, 0.0), 0.0)
        qaug.append(jnp.concatenate([qh[h], neg.astype(BF16)], axis=1))

    row = lax.broadcasted_iota(jnp.int32, (tq, tq), 0)
    col = lax.broadcasted_iota(jnp.int32, (tq, tq), 1)
    causal = col <= row

    def tile(j, states, diag):
        start = pl.multiple_of(j * tq, tq)
        kt = kaug_ref[:, pl.ds(start, tq)]
        vt = v_ref[:, pl.ds(start, tq)]
        new = []
        for h in range(2):
            s = _dot(qaug[h], kt) + bias_ref[h, jnp.minimum(i - j, 2)]
            if diag:
                s = jnp.where(causal, s, NEG)
            new.append(_softmax_tile(s, vt, states[h]))
        return tuple(new)

    states = tile(i, (_softmax_init(tq), _softmax_init(tq)), True)
    states = lax.fori_loop(0, i, lambda j, st: tile(j, st, False), states)
    _softmax_finish(states, o_ref)


def _moba_attention(qb, kbt, vbt, qf, kt, bias_tiles, n_pairs, col0):
    b, t, _ = qb.shape
    tq = bias_tiles.shape[-1]
    nb = t // MOBA_BLOCK
    q_spec, kv_spec, o_spec = _pair_specs(t, tq, col0)
    return pl.pallas_call(
        functools.partial(_moba_kernel, tq=tq, nb=nb),
        grid=(b, n_pairs, t // tq),
        in_specs=[q_spec, kv_spec, kv_spec, q_spec, kv_spec,
                  pl.BlockSpec((2, 3, tq, tq), lambda bi, p, i: (p, 0, 0, 0))],
        out_specs=o_spec,
        out_shape=jax.ShapeDtypeStruct((b, t, n_pairs * LANES), BF16),
        scratch_shapes=[pltpu.VMEM((2 * nb, LANES), F32), pltpu.VMEM((2 * LANES, t), BF16)],
        compiler_params=_params("parallel", "parallel", "arbitrary"),
        name="moba_attention",
    )(qb, kbt, vbt, qf, kt, bias_tiles)


def _page_copies(pt_ref, b, slot, n_pages, pools_bufs_sems):
    cps = []
    for pool, buf, sem in pools_bufs_sems:
        for p in range(n_pages):
            cps.append(pltpu.make_async_copy(pool.at[pt_ref[b, p]], buf.at[slot, p], sem.at[slot, p]))
    return cps


def _gather_pipeline(pt_ref, b, nseq, n_pages, pools_bufs_sems):
    slot = b % 2

    @pl.when(b == 0)
    def _():
        for cp in _page_copies(pt_ref, b, slot, n_pages, pools_bufs_sems):
            cp.start()

    @pl.when(b + 1 < nseq)
    def _():
        for cp in _page_copies(pt_ref, b + 1, 1 - slot, n_pages, pools_bufs_sems):
            cp.start()

    for cp in _page_copies(pt_ref, b, slot, n_pages, pools_bufs_sems):
        cp.wait()
    return slot


def _own_column(x, b):
    lane = lax.broadcasted_iota(jnp.int32, x.shape, x.ndim - 1)
    return jnp.sum(jnp.where(lane == b, x, 0.0), axis=x.ndim - 1, keepdims=True)


def _page_logits(kbuf, slot, q_col, z_ref, n_pages, pages_per_chunk):
    nh = q_col.shape[0]
    for g in range(0, nh, PV_HEAD_GROUP):
        rows = slice(g, g + PV_HEAD_GROUP)
        q_bc = jnp.broadcast_to(q_col[rows], (PV_HEAD_GROUP, HEAD_DIM, PAGE_SIZE))
        for p in range(n_pages):
            z_ref[rows, p * PAGE_SIZE:(p + 1) * PAGE_SIZE] = jnp.sum(kbuf[slot, p, rows] * q_bc, axis=1)
    chunk = pages_per_chunk * PAGE_SIZE
    return [z_ref[:, c * chunk:(c + 1) * chunk] for c in range(n_pages // pages_per_chunk)]


def _page_values(vbuf, slot, w_chunks, n_pages, pages_per_chunk, finish, o_ref):
    nh = w_chunks[0].shape[0]
    for g in range(0, nh, PV_HEAD_GROUP):
        rows = slice(g, g + PV_HEAD_GROUP)
        acc = jnp.zeros((PV_HEAD_GROUP, HEAD_DIM, PAGE_SIZE), F32)
        for p in range(n_pages):
            c, r = divmod(p, pages_per_chunk)
            w = w_chunks[c][rows, r * PAGE_SIZE:(r + 1) * PAGE_SIZE]
            acc = acc + vbuf[slot, p, rows] * w[:, None, :]
        o_ref[rows, :] = finish(rows, jnp.sum(acc, axis=2))


def _dec_even_kernel(b, pt_ref, qt_ref, knt_ref, dbias_ref, b0_ref, tri_ref, vn_ref, kc_ref, vc_ref,
                     o_ref, z_ref, kbuf, vbuf, ksem, vsem, *, n_pages, n_sb, chunk):
    slot = _gather_pipeline(pt_ref, b, qt_ref.shape[2], n_pages, [(kc_ref, kbuf, ksem), (vc_ref, vbuf, vsem)])
    ppc = chunk // PAGE_SIZE
    nc = n_pages // ppc
    nh = qt_ref.shape[0]
    n_mb = nh - n_sb

    qt = qt_ref[...]
    zs = _page_logits(kbuf, slot, _own_column(qt, b), z_ref, n_pages, ppc)
    z_new = _own_column(jnp.sum(qt * knt_ref[...], axis=1), b)

    tri = tri_ref[...]
    run = jnp.zeros((n_sb, 1), F32)
    w_sb = [None] * nc
    for c in reversed(range(nc)):
        lb, l1 = _log_sigmoid_parts(zs[c][0:n_sb])
        hi, lo = _split2(l1)
        tail = _dot(hi, tri) + _dot(lo, tri)
        w_sb[c] = jnp.exp(lb + tail + run)
        run = run + tail[:, 0:1] + l1[:, 0:1]

    sc = [jnp.sum(zs[n][n_sb:], axis=1, keepdims=True) * (1.0 / MOBA_BLOCK) for n in range(nc)]
    lg = []
    for n in range(nc):
        rank = jnp.zeros((n_mb, 1), F32)
        for m in range(nc):
            if m != n:
                beats = (sc[m] > sc[n]) | ((sc[m] == sc[n]) & (m < n))
                rank = rank + jnp.where(beats, 1.0, 0.0)
        mask = jnp.where(rank < MOBA_TOPK, 0.0, NEG)
        lg.append(zs[n][n_sb:] + dbias_ref[:, n * chunk:(n + 1) * chunk] + mask)
    lg_new = z_new[n_sb:] + b0_ref[...]
    m = lg_new
    for n in range(nc):
        m = jnp.maximum(m, jnp.max(lg[n], axis=1, keepdims=True))
    p_new = jnp.exp(lg_new - m)
    l = p_new
    w_all = []
    for c in range(nc):
        p = jnp.exp(lg[c] - m)
        l = l + jnp.sum(p, axis=1, keepdims=True)
        w_all.append(jnp.concatenate([w_sb[c], p], axis=0))
    p_new_all = jnp.concatenate([jnp.zeros((n_sb, 1), F32), p_new], axis=0)
    inv_l = jnp.concatenate([jnp.ones((n_sb, 1), F32), 1.0 / l], axis=0)
    finish = lambda rows, acc: (acc + p_new_all[rows] * vn_ref[rows, :]) * inv_l[rows]
    _page_values(vbuf, slot, w_all, n_pages, ppc, finish, o_ref)


def _dec_odd_kernel(b, pt_ref, qt_ref, knt_ref, lfn_ref, tri_ref, vn_ref, kc_ref, vc_ref, lc_ref,
                    o_ref, z_ref, kbuf, vbuf, lbuf, ksem, vsem, lsem, *, n_pages, chunk):
    slot = _gather_pipeline(pt_ref, b, qt_ref.shape[2], n_pages,
                            [(kc_ref, kbuf, ksem), (vc_ref, vbuf, vsem), (lc_ref, lbuf, lsem)])
    ppc = chunk // PAGE_SIZE
    nc = n_pages // ppc

    qt = qt_ref[...]
    zs = _page_logits(kbuf, slot, _own_column(qt, b), z_ref, n_pages, ppc)
    lg_new = _own_column(jnp.sum(qt * knt_ref[...], axis=1), b)
    tri = tri_ref[...]
    run = _own_column(lfn_ref[...], b)
    lg = [None] * nc
    for c in reversed(range(nc)):
        lf_c = jnp.concatenate([lbuf[slot, p] for p in range(c * ppc, (c + 1) * ppc)], axis=1)
        a, b3, d = _split3(lf_c)
        tail = _dot(a, tri) + _dot(b3, tri) + _dot(d, tri)
        lg[c] = zs[c] + tail + run
        run = run + tail[:, 0:1] + lf_c[:, 0:1]
    m = lg_new
    for c in range(nc):
        m = jnp.maximum(m, jnp.max(lg[c], axis=1, keepdims=True))
    p_new = jnp.exp(lg_new - m)
    l = p_new
    w_all = []
    for c in range(nc):
        p = jnp.exp(lg[c] - m)
        l = l + jnp.sum(p, axis=1, keepdims=True)
        w_all.append(p)
    finish = lambda rows, acc: (acc + p_new[rows] * vn_ref[rows, :]) / l[rows]
    _page_values(vbuf, slot, w_all, n_pages, ppc, finish, o_ref)


def _page_scratch(n_pages, nh, with_logf):
    page = (2, n_pages, nh, HEAD_DIM, PAGE_SIZE)
    bufs = [pltpu.VMEM(page, F32), pltpu.VMEM(page, F32)]
    if with_logf:
        bufs.append(pltpu.VMEM((2, n_pages, nh, PAGE_SIZE), F32))
    logits = pltpu.VMEM((nh, n_pages * PAGE_SIZE), F32)
    return [logits] + bufs + [pltpu.SemaphoreType.DMA((2, n_pages)) for _ in bufs]


def _attend_and_decode_kernel(pt_ref, *refs, host, n_host_in, n_host_scratch, dec, n_dec_in, steps_per_seq):
    host_in = refs[:n_host_in]
    dec_in = refs[n_host_in:n_host_in + n_dec_in]
    host_out, dec_out = refs[n_host_in + n_dec_in:n_host_in + n_dec_in + 2]
    scratch = refs[n_host_in + n_dec_in + 2:]
    host_scratch, dec_scratch = scratch[:n_host_scratch], scratch[n_host_scratch:]
    step = ((pl.program_id(0) * pl.num_programs(1) + pl.program_id(1)) * pl.num_programs(2)
            + pl.program_id(2))

    @pl.when(step % steps_per_seq == 0)
    def _():
        dec(step // steps_per_seq, pt_ref, *dec_in, dec_out, *dec_scratch)

    host(*host_in, host_out, *host_scratch)


def _attend_and_decode(name, page_table, dec, dec_consts, vn, pools, scratch, *, grid, host, host_specs,
                       host_out_spec, host_out_shape, host_args, host_scratch):
    nseq = page_table.shape[0]
    nh = vn.shape[1]
    steps = math.prod(grid)
    assert steps % nseq == 0
    spp = steps // nseq
    per_seq = pl.BlockSpec((None, nh, HEAD_DIM),
                           lambda bi, p, i, pt: (((bi * grid[1] + p) * grid[2] + i) // spp, 0, 0))
    dec_specs = ([_const_spec(a.shape) for a in dec_consts] + [per_seq]
                 + [pl.BlockSpec(memory_space=pl.ANY) for _ in pools])
    grid_spec = pltpu.PrefetchScalarGridSpec(
        num_scalar_prefetch=1, grid=grid, in_specs=list(host_specs) + dec_specs,
        out_specs=[host_out_spec, per_seq], scratch_shapes=list(host_scratch) + list(scratch))
    return pl.pallas_call(
        functools.partial(_attend_and_decode_kernel, host=host, n_host_in=len(host_args),
                          n_host_scratch=len(host_scratch), dec=dec,
                          n_dec_in=len(dec_specs), steps_per_seq=spp),
        grid_spec=grid_spec,
        out_shape=[host_out_shape, jax.ShapeDtypeStruct((nseq, nh, HEAD_DIM), F32)],
        compiler_params=_params("arbitrary", "arbitrary", "arbitrary"),
        name=name,
    )(page_table, *host_args, *dec_consts, vn, *pools)


def _tri(n, keep):
    r = lax.broadcasted_iota(jnp.int32, (n, n), 0)
    c = lax.broadcasted_iota(jnp.int32, (n, n), 1)
    return keep(r, c).astype(BF16)


def _token_major(xt, nh):
    g, _, seq = xt.shape
    return jnp.transpose(xt.reshape(g, nh, HEAD_DIM, seq), (0, 3, 1, 2))


def kernel(x_prompt, x_sample, cache_k_even, cache_v_even, cache_k_odd, cache_v_odd, cache_logf_odd, page_table, attn_norm, ffn_norm, w_in_even, w_out_even, q_norm_even, k_norm_even, rel_bias, w_in_odd, b_f_odd, w_out_odd, q_norm_odd, k_norm_odd, w_gate, w_up, w_down):
    bsz, seq, d = x_prompt.shape
    nseq, dec_seq, _ = x_sample.shape
    depth = attn_norm.shape[0]
    page = cache_k_even.shape[2]
    nh = cache_k_even.shape[3]
    dm = nh * HEAD_DIM
    n_mb = rel_bias.shape[1]
    n_sb = nh - n_mb
    half = n_sb * HEAD_DIM
    n_pages = page_table.shape[1]
    past = n_pages * page
    assert dec_seq == 1 and page == PAGE_SIZE and n_sb == n_mb and n_sb % PV_HEAD_GROUP == 0
    assert seq % MOBA_TILE == 0 and past % MOBA_BLOCK == 0 and half % MXU_DIM == 0 and MOBA_TILE % MOBA_BLOCK == 0

    tri_after = _tri(MOBA_BLOCK, lambda r, c: r > c)
    tri_incl = _tri(MOBA_BLOCK, lambda r, c: r <= c)
    seg = _tri(MXU_DIM, lambda r, c: (r // HEAD_DIM) == (c // HEAD_DIM)) * (1.0 / HEAD_DIM)

    pool_view = lambda cache: jnp.transpose(cache, (0, 2, 3, 1))
    heads_first = lambda a: jnp.transpose(a.reshape(nseq, nh, HEAD_DIM), (1, 2, 0))

    xp = x_prompt.reshape(bsz * seq, d)
    xs = x_sample.reshape(nseq, d)
    bias_tiles, dec_bias = _bias_tables(rel_bias, past, MOBA_TILE)
    dec_bias = dec_bias.reshape(n_mb, past)
    b0 = rel_bias[0].reshape(n_mb, 1)

    outs = {name: [] for name in ("ek_p", "ev_p", "ek_s", "ev_s", "ok_p", "ov_p", "of_p", "ok_s", "ov_s", "of_s")}
    r3 = lambda a: a.reshape(bsz, seq, a.shape[-1])
    sample_rows = lambda xt: jnp.transpose(xt[0]).reshape(nseq, nh, HEAD_DIM)
    for l in range(depth):
        i = l // 2
        g_attn = attn_norm[l].reshape(1, d)
        if l % 2 == 0:
            w = w_in_even[i]
            cols = lambda j: w[:, j * half:(j + 1) * half]
            wq = jnp.concatenate([cols(0), cols(3)], axis=1).astype(BF16)
            wkv = jnp.transpose(jnp.concatenate([cols(1), cols(4), cols(2), cols(5)], axis=1)).astype(BF16)
            gq = jnp.tile(q_norm_even[i], n_mb).reshape(1, half)
            gk = jnp.tile(k_norm_even[i], n_mb).reshape(half, 1)
            kt, vt, kbt, vbt, qb, qf = _proj_even(xp, g_attn, wq, wkv, gq, gk, seg, bsz)
            o_mb = _moba_attention(r3(qb), kbt, vbt, r3(qf), kt, bias_tiles, n_mb // 2, half // LANES)
            outs["ek_p"].append(_token_major(kt, nh))
            outs["ev_p"].append(_token_major(vt, nh))
            kt_s, vt_s, _, _, _, qf_s = _proj_even(xs, g_attn, wq, wkv, gq, gk, seg, 1)
            o_sb, o_s = _attend_and_decode(
                "sb_attention_decode_even", page_table,
                functools.partial(_dec_even_kernel, n_pages=n_pages, n_sb=n_sb, chunk=MOBA_BLOCK),
                (heads_first(qf_s), kt_s.reshape(nh, HEAD_DIM, nseq), dec_bias, b0, tri_after), sample_rows(vt_s),
                (pool_view(cache_k_even[i]), pool_view(cache_v_even[i])), _page_scratch(n_pages, nh, False),
                **_sb_host(r3(qb), kbt, vbt, tri_after, n_sb // 2, SB_TILE))
            o_s = o_s.reshape(nseq, dm)
            outs["ek_s"].append(_token_major(kt_s, nh).reshape(nseq, 1, nh, HEAD_DIM))
            outs["ev_s"].append(_token_major(vt_s, nh).reshape(nseq, 1, nh, HEAD_DIM))
            w_out = w_out_even[i].astype(BF16)
            o_p = (o_sb.reshape(bsz * seq, half), o_mb.reshape(bsz * seq, half), 0, 0)
        else:
            w = w_in_odd[i]
            wq = w[:, 0:dm].astype(BF16)
            wkv = jnp.pad(jnp.transpose(w[:, dm:]), ((0, LANES - nh), (0, 0))).astype(BF16)
            b_f = jnp.pad(b_f_odd[i], (0, LANES - nh)).reshape(LANES, 1)
            gq = jnp.tile(q_norm_odd[i], nh).reshape(1, dm)
            gk = jnp.tile(k_norm_odd[i], nh).reshape(dm, 1)
            kt, vt, kbt, vbt, qb, _, lft = _proj_odd(xp, g_attn, wq, wkv, gq, gk, b_f, seg, nh, bsz)
            c_t = _cumsum_lanes(lft, tri_incl)
            outs["ok_p"].append(_token_major(kt, nh))
            outs["ov_p"].append(_token_major(vt, nh))
            outs["of_p"].append(jnp.transpose(lft, (0, 2, 1)))
            kt_s, vt_s, _, _, _, qf_s, lft_s = _proj_odd(xs, g_attn, wq, wkv, gq, gk, b_f, seg, nh, 1)
            o, o_s = _attend_and_decode(
                "fox_attention_decode_odd", page_table,
                functools.partial(_dec_odd_kernel, n_pages=n_pages, chunk=MOBA_BLOCK),
                (heads_first(qf_s), kt_s.reshape(nh, HEAD_DIM, nseq), lft_s[0], tri_after), sample_rows(vt_s),
                (pool_view(cache_k_odd[i]), pool_view(cache_v_odd[i]), jnp.transpose(cache_logf_odd[i], (0, 2, 1))),
                _page_scratch(n_pages, nh, True),
                **_fox_host(r3(qb), kbt, vbt, c_t.reshape(bsz, nh // 2, 2, seq), FOX_TILE))
            o_s = o_s.reshape(nseq, dm)
            outs["ok_s"].append(_token_major(kt_s, nh).reshape(nseq, 1, nh, HEAD_DIM))
            outs["ov_s"].append(_token_major(vt_s, nh).reshape(nseq, 1, nh, HEAD_DIM))
            outs["of_s"].append(jnp.transpose(lft_s[0]).reshape(nseq, 1, nh))
            w_out = w_out_odd[i].astype(BF16)
            o2 = o.reshape(bsz * seq, dm)
            o_p = (o2, o2, 0, 1)
        g_ffn = ffn_norm[l].reshape(1, d)
        wg, wu, wd = w_gate[l].astype(BF16), w_up[l].astype(BF16), w_down[l].astype(BF16)
        xp = _post(xp, *o_p, w_out, g_ffn, wg, wu, wd)
        xs = _post(xs, o_s, o_s, 0, 1, w_out, g_ffn, wg, wu, wd)

    st = lambda name: jnp.stack(outs[name])
    return (xp.reshape(bsz, seq, d), xs.reshape(nseq, 1, d),
            st("ek_p"), st("ev_p"), st("ek_s"), st("ev_s"),
            st("ok_p"), st("ov_p"), st("of_p"), st("ok_s"), st("ov_s"), st("of_s"))
```
